```python
import jax, jax.numpy as jnp
from jax import lax
import numpy as np

D_MODEL = 1024
BATCH = 8
SEQ = 2048
DEPTH = 4

CHUNK = 128
EPS = 1e-6
GM_HEADS = 4
GM_HEAD_DIM = 64
GM_WIDTH = GM_HEADS * GM_HEAD_DIM
HG_HEADS = 4
HG_HEAD_DIM = 64
HG_WIDTH = HG_HEADS * HG_HEAD_DIM
MLA_HEADS = 8
QK_NOPE = 64
QK_ROPE = 32
QK_DIM = QK_NOPE + QK_ROPE
V_DIM = 64
Q_LORA = 256
KV_LORA = 128
MLA_WIDTH = MLA_HEADS * V_DIM
ROPE_THETA = 10000.0
Q_BLOCK = 128
D_MIX = GM_WIDTH + HG_WIDTH + MLA_WIDTH
D_FF = 4 * D_MODEL
IN_SPLITS = (GM_WIDTH, GM_WIDTH, HG_WIDTH, HG_WIDTH, HG_WIDTH, HG_WIDTH, Q_LORA, KV_LORA, QK_ROPE)
D_IN = 2 * GM_WIDTH + 4 * HG_WIDTH + Q_LORA + KV_LORA + QK_ROPE

kernel_name = "hybrid_gmlp_hgrn2_mla_trunk"


def rms_norm(x, gain):
    xf = x.astype(jnp.float32)
    y = xf * lax.rsqrt(jnp.mean(xf * xf, axis=-1, keepdims=True) + EPS)
    return (y * gain.astype(jnp.float32)).astype(x.dtype)


def head_rms_norm(x, n_heads, gain):
    shp = x.shape
    xh = x.reshape(shp[:-1] + (n_heads, shp[-1] // n_heads))
    return rms_norm(xh, gain.reshape(n_heads, -1)).reshape(shp)


def rope(x, positions):
    half = QK_ROPE // 2
    inv_freq = ROPE_THETA ** (-jnp.arange(half, dtype=jnp.float32) / half)
    ang = positions.astype(jnp.float32)[:, :, None, None] * inv_freq
    cos, sin = jnp.cos(ang), jnp.sin(ang)
    xf = x.astype(jnp.float32)
    x1, x2 = xf[..., :half], xf[..., half:]
    return jnp.concatenate([x1 * cos - x2 * sin, x2 * cos + x1 * sin], axis=-1).astype(x.dtype)


def chunked_spatial_gating(u_raw, v_raw, v_gain, w_s, b_s, out_gain):
    B, S, _ = u_raw.shape
    nc = S // CHUNK
    u = jax.nn.gelu(u_raw)
    v = head_rms_norm(jax.nn.gelu(v_raw), GM_HEADS, v_gain)
    v = v.reshape(B, nc, CHUNK, GM_HEADS, GM_HEAD_DIM)
    causal = jnp.tril(jnp.ones((CHUNK, CHUNK), dtype=bool))
    w = jnp.where(causal, w_s, 0).astype(v.dtype)
    y = jnp.einsum('hts,bnshd->bnthd', w, v) + b_s.T[:, :, None].astype(v.dtype)
    out = u * y.reshape(B, S, GM_WIDTH)
    return head_rms_norm(out, GM_HEADS, out_gain)


def hgrn2(q_raw, f_raw, i_raw, g_raw, lower_bound, out_gain):
    B, S, _ = q_raw.shape
    nc = S // CHUNK
    f32 = jnp.float32
    q = jax.nn.silu(q_raw.astype(f32))
    lb = lower_bound.astype(f32)
    f = lb + (1.0 - lb) * jax.nn.sigmoid(f_raw.astype(f32))
    k = 1.0 - f
    log_f = jnp.log(f)

    def to_chunks(t):
        return t.reshape(B, nc, CHUNK, HG_HEADS, HG_HEAD_DIM).transpose(1, 0, 3, 2, 4)

    qc, kc, vc, lc = to_chunks(q), to_chunks(k), to_chunks(i_raw.astype(f32)), to_chunks(log_f)
    bc = jnp.cumsum(lc, axis=-2)
    causal = jnp.tril(jnp.ones((CHUNK, CHUNK), dtype=bool))[:, :, None]

    def step(state, inp):
        q_, k_, v_, b_ = inp
        inter = jnp.einsum('bhtk,bhkv->bhtv', q_ * jnp.exp(b_), state)
        diff = b_[:, :, :, None, :] - b_[:, :, None, :, :]
        decay = jnp.exp(jnp.where(causal, diff, -jnp.inf))
        scores = jnp.einsum('bhtk,bhtsk,bhsk->bhts', q_, decay, k_)
        intra = jnp.einsum('bhts,bhsv->bhtv', scores, v_)
        b_last = b_[:, :, -1:, :]
        new_state = (jnp.exp(b_last[:, :, 0, :, None]) * state
                     + jnp.einsum('bhsk,bhsv->bhkv', k_ * jnp.exp(b_last - b_), v_))
        return new_state, inter + intra

    s0 = jnp.zeros((B, HG_HEADS, HG_HEAD_DIM, HG_HEAD_DIM), f32)
    _, o = lax.scan(step, s0, (qc, kc, vc, bc))
    o = o.transpose(1, 0, 3, 2, 4).reshape(B, S, HG_WIDTH)
    o = head_rms_norm(o, HG_HEADS, out_gain) * jax.nn.silu(g_raw.astype(f32))
    return o.astype(q_raw.dtype)


def mla(cq_raw, ckv_raw, kpe_raw, positions, q_a_gain, w_uq, kv_a_gain, w_ukv,
        q_gain, k_gain, out_gain):
    B, S, _ = cq_raw.shape
    q = (rms_norm(cq_raw, q_a_gain) @ w_uq).reshape(B, S, MLA_HEADS, QK_DIM)
    kv = (rms_norm(ckv_raw, kv_a_gain) @ w_ukv).reshape(B, S, MLA_HEADS, QK_NOPE + V_DIM)
    k_nope, v = kv[..., :QK_NOPE], kv[..., QK_NOPE:]
    k_pe = jnp.broadcast_to(kpe_raw[:, :, None, :], (B, S, MLA_HEADS, QK_ROPE))
    k = jnp.concatenate([k_nope, k_pe], axis=-1)
    q = rms_norm(q, q_gain)
    k = rms_norm(k, k_gain)
    q = jnp.concatenate([q[..., :QK_NOPE], rope(q[..., QK_NOPE:], positions)], axis=-1)
    k = jnp.concatenate([k[..., :QK_NOPE], rope(k[..., QK_NOPE:], positions)], axis=-1)
    q = q.transpose(0, 2, 1, 3)
    k = k.transpose(0, 2, 1, 3)
    v = v.transpose(0, 2, 1, 3)
    scale = QK_DIM ** -0.5
    outs = []
    for blk in range(S // Q_BLOCK):
        lo, hi = blk * Q_BLOCK, (blk + 1) * Q_BLOCK
        s = jnp.einsum('bhqd,bhkd->bhqk', q[:, :, lo:hi], k[:, :, :hi]).astype(jnp.float32) * scale
        mask = (lo + jnp.arange(Q_BLOCK))[:, None] >= jnp.arange(hi)[None, :]
        p = jax.nn.softmax(jnp.where(mask, s, -jnp.inf), axis=-1).astype(v.dtype)
        outs.append(jnp.einsum('bhqk,bhkd->bhqd', p, v[:, :, :hi]))
    o = jnp.concatenate(outs, axis=2).transpose(0, 2, 1, 3).reshape(B, S, MLA_WIDTH)
    return head_rms_norm(o, MLA_HEADS, out_gain)


def setup_inputs(seed: int = 0) -> dict:
    key = jax.random.key(seed)
    ks = jax.random.split(key, 24)
    f32 = jnp.float32

    def nrm(k, shape, scale):
        return jax.random.normal(k, shape, f32) * scale

    def gain(k, shape):
        return 1.0 + 0.02 * jax.random.normal(k, shape, f32)

    x = jax.random.normal(ks[0], (BATCH, SEQ, D_MODEL), f32)
    offsets = jax.random.randint(ks[1], (BATCH, 1), 0, 1024, dtype=jnp.int32)
    positions = offsets + jnp.arange(SEQ, dtype=jnp.int32)[None, :]
    return {
        "x": x,
        "positions": positions,
        "norm1_gain": gain(ks[2], (DEPTH, D_MODEL)),
        "w_in": nrm(ks[3], (DEPTH, D_MODEL, D_IN), D_MODEL ** -0.5),
        "gm_v_gain": gain(ks[4], (DEPTH, GM_WIDTH)),
        "gm_w_s": nrm(ks[5], (DEPTH, GM_HEADS, CHUNK, CHUNK), CHUNK ** -0.5),
        "gm_b_s": gain(ks[6], (DEPTH, GM_HEADS, CHUNK)),
        "gm_out_gain": gain(ks[7], (DEPTH, GM_WIDTH)),
        "hg_lower_bound": nrm(ks[8], (DEPTH, HG_WIDTH), 0.1),
        "hg_out_gain": gain(ks[9], (DEPTH, HG_WIDTH)),
        "mla_q_a_gain": gain(ks[10], (DEPTH, Q_LORA)),
        "mla_w_uq": nrm(ks[11], (DEPTH, Q_LORA, MLA_HEADS * QK_DIM), Q_LORA ** -0.5),
        "mla_kv_a_gain": gain(ks[12], (DEPTH, KV_LORA)),
        "mla_w_ukv": nrm(ks[13], (DEPTH, KV_LORA, MLA_HEADS * (QK_NOPE + V_DIM)), KV_LORA ** -0.5),
        "mla_q_gain": gain(ks[14], (DEPTH, QK_DIM)),
        "mla_k_gain": gain(ks[15], (DEPTH, QK_DIM)),
        "mla_out_gain": gain(ks[16], (DEPTH, MLA_WIDTH)),
        "w_out": nrm(ks[17], (DEPTH, D_MIX, D_MODEL), (2 * D_MIX) ** -0.5),
        "norm2_gain": gain(ks[18], (DEPTH, D_MODEL)),
        "w_ff1": nrm(ks[19], (DEPTH, D_MODEL, D_FF), D_MODEL ** -0.5),
        "w_ff2": nrm(ks[20], (DEPTH, D_FF, D_MODEL), (2 * D_FF) ** -0.5),
    }


def reference(x, positions, norm1_gain, w_in, gm_v_gain, gm_w_s, gm_b_s, gm_out_gain,
              hg_lower_bound, hg_out_gain, mla_q_a_gain, mla_w_uq, mla_kv_a_gain, mla_w_ukv,
              mla_q_gain, mla_k_gain, mla_out_gain, w_out, norm2_gain, w_ff1, w_ff2):
    lb_soft = jax.nn.softmax(hg_lower_bound.astype(jnp.float32), axis=0)
    lower_bounds = jnp.cumsum(lb_soft, axis=0) - lb_soft[0]
    split_points = [int(s) for s in np.cumsum(IN_SPLITS)[:-1]]
    for l in range(DEPTH):
        h = rms_norm(x, norm1_gain[l])
        proj = h @ w_in[l]
        a_u, a_v, b_q, b_f, b_i, b_g, c_q, c_kv, c_pe = jnp.split(proj, split_points, axis=-1)
        y_a = chunked_spatial_gating(a_u, a_v, gm_v_gain[l], gm_w_s[l], gm_b_s[l], gm_out_gain[l])
        y_b = hgrn2(b_q, b_f, b_i, b_g, lower_bounds[l], hg_out_gain[l])
        y_c = mla(c_q, c_kv, c_pe, positions, mla_q_a_gain[l], mla_w_uq[l], mla_kv_a_gain[l],
                  mla_w_ukv[l], mla_q_gain[l], mla_k_gain[l], mla_out_gain[l])
        mix = jnp.concatenate([y_a, y_b, y_c], axis=-1)
        x = x + mix @ w_out[l]
        h2 = rms_norm(x, norm2_gain[l])
        x = x + jnp.square(jax.nn.relu(h2 @ w_ff1[l])) @ w_ff2[l]
    return x
```

```python
import functools

import jax
import jax.numpy as jnp
from jax import lax
from jax.experimental import pallas as pl
from jax.experimental.pallas import tpu as pltpu

F32 = jnp.float32
BF16 = jnp.bfloat16

D_MODEL = 1024
DEPTH = 4
CHUNK = 128
EPS = 1e-6
GM_HEADS = 4
HG_HEADS = 4
HEAD64 = 64
PAIR = 2 * HEAD64
MLA_HEADS = 8
QK_NOPE = 64
QK_ROPE = 32
QK_DIM = QK_NOPE + QK_ROPE
Q_LORA = 256
KV_LORA = 128
MLA_WIDTH = MLA_HEADS * HEAD64
ROPE_THETA = 10000.0
D_FF = 4 * D_MODEL
LANES = 128
D_IN_PAD = 2048
MLA_PAD = MLA_HEADS * LANES

VMEM_LIMIT = 48 * 1024 * 1024

TM_IN = 512
TM_GM = 512
TM_MLA = 512
TQ = 256
TM_FFN = 512
TF_FFN = 1024


def _params(*sem):
    return pltpu.CompilerParams(dimension_semantics=sem, vmem_limit_bytes=VMEM_LIMIT)


def _dot(a, b):
    return jnp.dot(a, b, preferred_element_type=F32)


def _dot_nt(a, b):
    return lax.dot_general(a, b, (((1,), (1,)), ((), ())), preferred_element_type=F32)


def _dot_tn(a, b):
    return lax.dot_general(a, b, (((0,), (0,)), ((), ())), preferred_element_type=F32)


def _split_dot(x, w_bf16, parts, w_on_left=False):
    acc = None
    rem = x
    for _ in range(parts):
        piece = rem.astype(BF16)
        term = _dot(w_bf16, piece) if w_on_left else _dot(piece, w_bf16)
        acc = term if acc is None else acc + term
        rem = rem - piece.astype(F32)
    return acc


def _head_mean_matrix(width):
    r = lax.broadcasted_iota(jnp.int32, (width, width), 0) // HEAD64
    c = lax.broadcasted_iota(jnp.int32, (width, width), 1) // HEAD64
    return jnp.where(r == c, 1.0 / HEAD64, 0.0).astype(BF16)


def _head_rms(x, gmat):
    ms = _split_dot(x * x, gmat, 2)
    return x * lax.rsqrt(ms + EPS)


def _in_proj_kernel(x_ref, g_ref, w_ref, o_ref):
    x = x_ref[...]
    ms = jnp.mean(x * x, axis=-1, keepdims=True)
    h = x * lax.rsqrt(ms + EPS) * g_ref[...]
    o_ref[...] = _dot(h.astype(BF16), w_ref[...])


def _in_proj(x, gain, w):
    t = x.shape[0]
    return pl.pallas_call(
        _in_proj_kernel,
        grid=(t // TM_IN,),
        in_specs=[
            pl.BlockSpec((TM_IN, D_MODEL), lambda i: (i, 0)),
            pl.BlockSpec((1, D_MODEL), lambda i: (0, 0)),
            pl.BlockSpec((D_MODEL, D_IN_PAD), lambda i: (0, 0)),
        ],
        out_specs=pl.BlockSpec((TM_IN, D_IN_PAD), lambda i: (i, 0)),
        out_shape=jax.ShapeDtypeStruct((t, D_IN_PAD), F32),
        compiler_params=_params("parallel"),
        name="in_proj",
    )(x, gain, w)


def _gm_kernel(u_ref, v_ref, vg_ref, w_ref, b_ref, og_ref, o_ref):
    width = GM_HEADS * HEAD64
    gmat = _head_mean_matrix(width)
    u = jax.nn.gelu(u_ref[...])
    v = _head_rms(jax.nn.gelu(v_ref[...]), gmat) * vg_ref[...]
    row = lax.broadcasted_iota(jnp.int32, (CHUNK, CHUNK), 0)
    col = lax.broadcasted_iota(jnp.int32, (CHUNK, CHUNK), 1)
    causal = row >= col
    lane_head = lax.broadcasted_iota(jnp.int32, (1, PAIR), 1) // HEAD64
    w_tril = [jnp.where(causal, w_ref[h], 0.0).astype(BF16) for h in range(GM_HEADS)]
    bias = b_ref[...]
    for c in range(TM_GM // CHUNK):
        rows = slice(c * CHUNK, (c + 1) * CHUNK)
        y_blocks = []
        for p in range(GM_HEADS // 2):
            vb = v[rows, p * PAIR:(p + 1) * PAIR]
            acc = None
            for hh in range(2):
                vm = jnp.where(lane_head == hh, vb, 0.0).astype(BF16)
                term = _dot(w_tril[2 * p + hh], vm)
                acc = term if acc is None else acc + term
            y_blocks.append(acc)
        y = jnp.concatenate(y_blocks, axis=-1) + bias
        out = u[rows] * y
        out = _head_rms(out, gmat) * og_ref[...]
        o_ref[rows, :] = out.astype(o_ref.dtype)


def _gm_mixer(proj, v_gain, w_s, bias_full, out_gain):
    t = proj.shape[0]
    width = GM_HEADS * HEAD64
    return pl.pallas_call(
        _gm_kernel,
        grid=(t // TM_GM,),
        in_specs=[
            pl.BlockSpec((TM_GM, width), lambda i: (i, 0)),
            pl.BlockSpec((TM_GM, width), lambda i: (i, 1)),
            pl.BlockSpec((1, width), lambda i: (0, 0)),
            pl.BlockSpec((GM_HEADS, CHUNK, CHUNK), lambda i: (0, 0, 0)),
            pl.BlockSpec((CHUNK, width), lambda i: (0, 0)),
            pl.BlockSpec((1, width), lambda i: (0, 0)),
        ],
        out_specs=pl.BlockSpec((TM_GM, width), lambda i: (i, 0)),
        out_shape=jax.ShapeDtypeStruct((t, width), BF16),
        compiler_params=_params("parallel"),
        name="gm_mixer",
    )(proj, proj, v_gain, w_s, bias_full, out_gain)


def _anchor_rows(b, m):
    n = b.shape[0]
    if m >= 8:
        pieces = []
        for blk in range(n // (2 * m)):
            a = blk * 2 * m + m - 1
            pieces.append(jnp.broadcast_to(b[a:a + 1, :], (2 * m, b.shape[1])))
        return pieces[0] if len(pieces) == 1 else jnp.concatenate(pieces, axis=0)
    pos = lax.broadcasted_iota(jnp.int32, (n, 1), 0) % (2 * m)
    r = b
    for p in range(2 * m):
        shift = p - (m - 1)
        if shift == 0:
            continue
        r = jnp.where(pos == p, pltpu.roll(b, shift % n, axis=0), r)
    return r


def _hg_kernel(layer, q_ref, f_ref, i_ref, g_ref, lb_ref, og_ref, o_ref, st_ref):
    width = HG_HEADS * HEAD64
    n_pairs = HG_HEADS // 2

    @pl.when(pl.program_id(1) == 0)
    def _():
        st_ref[...] = jnp.zeros_like(st_ref)

    lbs = lb_ref[...]
    e = jnp.exp(lbs - jnp.max(lbs, axis=0, keepdims=True))
    soft = e / jnp.sum(e, axis=0, keepdims=True)
    lb = jnp.zeros((1, width), F32)
    for i in range(1, layer + 1):
        lb = lb + soft[i:i + 1, :]

    qr = q_ref[...]
    q = qr * jax.nn.sigmoid(qr)
    f = lb + (1.0 - lb) * jax.nn.sigmoid(f_ref[...])
    kk = 1.0 - f
    lf = jnp.log(f)
    v = i_ref[...]

    row = lax.broadcasted_iota(jnp.int32, (CHUNK, CHUNK), 0)
    col = lax.broadcasted_iota(jnp.int32, (CHUNK, CHUNK), 1)
    tri = jnp.where(row >= col, 1.0, 0.0).astype(BF16)
    b = _split_dot(lf, tri, 3, w_on_left=True)

    lane_head = lax.broadcasted_iota(jnp.int32, (1, PAIR), 1) // HEAD64
    t_idx = lax.broadcasted_iota(jnp.int32, (CHUNK, 1), 0)

    def pair(x, p):
        return x[:, p * PAIR:(p + 1) * PAIR]

    def head_only(xb, hh):
        return jnp.where(lane_head == hh, xb, 0.0).astype(BF16)

    scores = [jnp.zeros((CHUNK, CHUNK), F32) for _ in range(HG_HEADS)]
    m = CHUNK // 2
    while m >= 1:
        upper = (t_idx % (2 * m)) >= m
        r = _anchor_rows(b, m)
        w = jnp.exp(jnp.minimum(jnp.where(upper, b - r, r - b), 0.0))
        u = jnp.where(upper, q, kk) * w
        mask = ((row // (2 * m)) == (col // (2 * m))) & ((row % (2 * m)) >= m) & ((col % (2 * m)) < m)
        for p in range(n_pairs):
            ub = pair(u, p)
            ubf = ub.astype(BF16)
            for hh in range(2):
                s = _dot_nt(ubf, head_only(ub, hh))
                scores[2 * p + hh] = scores[2 * p + hh] + jnp.where(mask, s, 0.0)
        m //= 2
    diag = row == col
    for p in range(n_pairs):
        qb = pair(q, p).astype(BF16)
        for hh in range(2):
            s = _dot_nt(qb, head_only(pair(kk, p), hh))
            scores[2 * p + hh] = scores[2 * p + hh] + jnp.where(diag, s, 0.0)

    qd = q * jnp.exp(b)
    b_last = b[CHUNK - 1:CHUNK, :]
    kd = kk * jnp.exp(jnp.minimum(b_last - b, 0.0))
    state_decay = jnp.exp(b_last)
    rr = lax.broadcasted_iota(jnp.int32, (PAIR, PAIR), 0) // HEAD64
    cc = lax.broadcasted_iota(jnp.int32, (PAIR, PAIR), 1) // HEAD64
    same_head = rr == cc

    o_blocks = []
    for p in range(n_pairs):
        st = st_ref[p]
        vb = pair(v, p)
        o = _dot_nt(pair(qd, p).astype(BF16), st.astype(BF16))
        for hh in range(2):
            o = o + _dot(scores[2 * p + hh].astype(BF16), head_only(vb, hh))
        o_blocks.append(o)
        upd = _dot_tn(vb.astype(BF16), pair(kd, p).astype(BF16))
        st_ref[p] = st * pair(state_decay, p) + jnp.where(same_head, upd, 0.0)
    o = jnp.concatenate(o_blocks, axis=-1)

    gmat = _head_mean_matrix(width)
    gr = g_ref[...]
    out = _head_rms(o, gmat) * og_ref[...] * (gr * jax.nn.sigmoid(gr))
    o_ref[...] = out.astype(o_ref.dtype)


def _hg_mixer(proj, lower_bound_raw, out_gain, layer, batch, seq):
    t = proj.shape[0]
    width = HG_HEADS * HEAD64
    nc = seq // CHUNK

    def col_block(j):
        return pl.BlockSpec((CHUNK, width), lambda bi, ci: (bi * nc + ci, j))

    return pl.pallas_call(
        functools.partial(_hg_kernel, layer),
        grid=(batch, nc),
        in_specs=[
            col_block(2), col_block(3), col_block(4), col_block(5),
            pl.BlockSpec((DEPTH, width), lambda bi, ci: (0, 0)),
            pl.BlockSpec((1, width), lambda bi, ci: (0, 0)),
        ],
        out_specs=pl.BlockSpec((CHUNK, width), lambda bi, ci: (bi * nc + ci, 0)),
        out_shape=jax.ShapeDtypeStruct((t, width), BF16),
        scratch_shapes=[pltpu.VMEM((HG_HEADS // 2, PAIR, PAIR), F32)],
        compiler_params=_params("parallel", "arbitrary"),
        name="hg_mixer",
    )(proj, proj, proj, proj, lower_bound_raw, out_gain)


def _rope_table_kernel(pos_ref, invf_ref, sign_ref, cos_ref, sin_ref):
    ang = pos_ref[...].astype(F32) * invf_ref[...]
    cos_ref[...] = jnp.cos(ang)
    sin_ref[...] = jnp.sin(ang) * sign_ref[...]


def _rope_tables(pos_col, invf_lane, sign_lane):
    t = pos_col.shape[0]
    tm = 2048
    return pl.pallas_call(
        _rope_table_kernel,
        grid=(t // tm,),
        in_specs=[
            pl.BlockSpec((tm, 1), lambda i: (i, 0)),
            pl.BlockSpec((1, LANES), lambda i: (0, 0)),
            pl.BlockSpec((1, LANES), lambda i: (0, 0)),
        ],
        out_specs=[pl.BlockSpec((tm, LANES), lambda i: (i, 0))] * 2,
        out_shape=[jax.ShapeDtypeStruct((t, LANES), F32)] * 2,
        compiler_params=_params("parallel"),
        name="rope_tables",
    )(pos_col, invf_lane, sign_lane)


def _swap_rope_halves(x):
    lane = lax.broadcasted_iota(jnp.int32, (1, LANES), 1)
    half = QK_ROPE // 2
    from_above = pltpu.roll(x, LANES - half, axis=1)
    from_below = pltpu.roll(x, half, axis=1)
    return jnp.where(lane < QK_NOPE + half, from_above, from_below)


def _mla_prep_kernel(cq_ref, ckv_ref, kpe_ref, cos_ref, sin_ref, qag_ref, kvag_ref,
                     wq_ref, wk_ref, wv_ref, qg_ref, kg_ref, q_out, k_out, v_out):
    def rms(x, gain):
        ms = jnp.mean(x * x, axis=-1, keepdims=True)
        return x * lax.rsqrt(ms + EPS) * gain

    hq = rms(cq_ref[...], qag_ref[...]).astype(BF16)
    hkv = rms(ckv_ref[...], kvag_ref[...]).astype(BF16)
    q = _dot(hq, wq_ref[...])
    kn = _dot(hkv, wk_ref[...])
    v_out[...] = _dot(hkv, wv_ref[...]).astype(v_out.dtype)
    kpe = kpe_ref[...]
    cos = cos_ref[...]
    sin = sin_ref[...]
    scale = QK_DIM ** -0.5

    def norm_rope(x, gain):
        ms = jnp.sum(x * x, axis=-1, keepdims=True) * (1.0 / QK_DIM)
        xn = x * lax.rsqrt(ms + EPS) * gain
        return xn * cos + _swap_rope_halves(xn) * sin

    for h in range(MLA_HEADS):
        lanes = slice(h * LANES, (h + 1) * LANES)
        q_out[:, lanes] = (norm_rope(q[:, lanes], qg_ref[...]) * scale).astype(q_out.dtype)
        k_out[:, lanes] = norm_rope(kn[:, lanes] + kpe, kg_ref[...]).astype(k_out.dtype)


def _mla_prep(proj, cos_t, sin_t, qa_gain, kva_gain, wq, wk, wv, q_gain, k_gain):
    t = proj.shape[0]
    full = lambda shape: pl.BlockSpec(shape, lambda i: (0,) * len(shape))
    out = jax.ShapeDtypeStruct((t, MLA_PAD), BF16)
    return pl.pallas_call(
        _mla_prep_kernel,
        grid=(t // TM_MLA,),
        in_specs=[
            pl.BlockSpec((TM_MLA, Q_LORA), lambda i: (i, 6)),
            pl.BlockSpec((TM_MLA, KV_LORA), lambda i: (i, 14)),
            pl.BlockSpec((TM_MLA, LANES), lambda i: (i, 15)),
            pl.BlockSpec((TM_MLA, LANES), lambda i: (i, 0)),
            pl.BlockSpec((TM_MLA, LANES), lambda i: (i, 0)),
            full((1, Q_LORA)), full((1, KV_LORA)),
            full((Q_LORA, MLA_PAD)), full((KV_LORA, MLA_PAD)), full((KV_LORA, MLA_PAD)),
            full((1, LANES)), full((1, LANES)),
        ],
        out_specs=[pl.BlockSpec((TM_MLA, MLA_PAD), lambda i: (i, 0))] * 3,
        out_shape=[out, out, out],
        compiler_params=_params("parallel"),
        name="mla_prep",
    )(proj, proj, proj, cos_t, sin_t, qa_gain, kva_gain, wq, wk, wv, q_gain, k_gain)


def _mla_attn_kernel(q_ref, k_ref, v_ref, og_ref, o_ref):
    qi = pl.program_id(1)
    row = lax.broadcasted_iota(jnp.int32, (TQ, TQ), 0)
    col = lax.broadcasted_iota(jnp.int32, (TQ, TQ), 1)
    causal = row >= col

    def attend(q, h, j, carry, masked):
        m_prev, l_prev, acc = carry
        lanes = slice(h * LANES, (h + 1) * LANES)
        start = pl.multiple_of(j * TQ, TQ)
        k = k_ref[pl.ds(start, TQ), lanes]
        v = v_ref[pl.ds(start, TQ), lanes]
        s = _dot_nt(q, k)
        if masked:
            s = jnp.where(causal, s, -jnp.inf)
        m_new = jnp.maximum(m_prev, jnp.max(s, axis=-1, keepdims=True))
        alpha = jnp.exp(m_prev - m_new)
        p = jnp.exp(s - m_new)
        l_new = alpha * l_prev + jnp.sum(p, axis=-1, keepdims=True)
        acc = alpha * acc + _dot(p.astype(BF16), v)
        return m_new, l_new, acc

    outs = []
    for h in range(MLA_HEADS):
        q = q_ref[:, h * LANES:(h + 1) * LANES]
        init = (jnp.full((TQ, 1), -jnp.inf, F32), jnp.zeros((TQ, 1), F32),
                jnp.zeros((TQ, LANES), F32))
        carry = lax.fori_loop(0, qi, lambda j, c: attend(q, h, j, c, False), init)
        _, l_fin, acc = attend(q, h, qi, carry, True)
        o = acc / l_fin
        ms = jnp.sum(o * o, axis=-1, keepdims=True) * (1.0 / HEAD64)
        outs.append(o * lax.rsqrt(ms + EPS))
    pairs = [outs[2 * p] + outs[2 * p + 1] for p in range(MLA_HEADS // 2)]
    o_ref[...] = (jnp.concatenate(pairs, axis=-1) * og_ref[...]).astype(o_ref.dtype)


def _mla_attn(q, k, v, out_gain, batch, seq):
    t = q.shape[0]
    nq = seq // TQ
    return pl.pallas_call(
        _mla_attn_kernel,
        grid=(batch, nq),
        in_specs=[
            pl.BlockSpec((TQ, MLA_PAD), lambda bi, qi: (bi * nq + qi, 0)),
            pl.BlockSpec((seq, MLA_PAD), lambda bi, qi: (bi, 0)),
            pl.BlockSpec((seq, MLA_PAD), lambda bi, qi: (bi, 0)),
            pl.BlockSpec((1, MLA_WIDTH), lambda bi, qi: (0, 0)),
        ],
        out_specs=pl.BlockSpec((TQ, MLA_WIDTH), lambda bi, qi: (bi * nq + qi, 0)),
        out_shape=jax.ShapeDtypeStruct((t, MLA_WIDTH), BF16),
        compiler_params=_params("parallel", "arbitrary"),
        name="mla_attn",
    )(q, k, v, out_gain)


def _out_ffn_kernel(x_ref, ya_ref, yb_ref, yc_ref, woa_ref, wob_ref, woc_ref, g_ref,
                    w1_ref, w2_ref, o_ref, h_ref):
    j = pl.program_id(1)

    @pl.when(j == 0)
    def _():
        xn = (x_ref[...] + _dot(ya_ref[...], woa_ref[...]) + _dot(yb_ref[...], wob_ref[...])
              + _dot(yc_ref[...], woc_ref[...]))
        ms = jnp.mean(xn * xn, axis=-1, keepdims=True)
        h_ref[...] = (xn * lax.rsqrt(ms + EPS) * g_ref[...]).astype(h_ref.dtype)
        o_ref[...] = xn

    a = jnp.maximum(_dot(h_ref[...], w1_ref[...]), 0.0)
    o_ref[...] += _dot((a * a).astype(BF16), w2_ref[...])


def _out_ffn(x, ya, yb, yc, woa, wob, woc, gain2, w1, w2):
    t = x.shape[0]
    wa, wb, wc = ya.shape[1], yb.shape[1], yc.shape[1]
    row = lambda width: pl.BlockSpec((TM_FFN, width), lambda i, j: (i, 0))
    const = lambda shape: pl.BlockSpec(shape, lambda i, j: (0, 0))
    return pl.pallas_call(
        _out_ffn_kernel,
        grid=(t // TM_FFN, D_FF // TF_FFN),
        in_specs=[
            row(D_MODEL), row(wa), row(wb), row(wc),
            const((wa, D_MODEL)), const((wb, D_MODEL)), const((wc, D_MODEL)),
            const((1, D_MODEL)),
            pl.BlockSpec((D_MODEL, TF_FFN), lambda i, j: (0, j)),
            pl.BlockSpec((TF_FFN, D_MODEL), lambda i, j: (j, 0)),
        ],
        out_specs=row(D_MODEL),
        out_shape=jax.ShapeDtypeStruct((t, D_MODEL), F32),
        scratch_shapes=[pltpu.VMEM((TM_FFN, D_MODEL), BF16)],
        compiler_params=_params("parallel", "arbitrary"),
        name="out_ffn",
    )(x, ya, yb, yc, woa, wob, woc, gain2, w1, w2)


def kernel(x, positions, norm1_gain, w_in, gm_v_gain, gm_w_s, gm_b_s, gm_out_gain,
           hg_lower_bound, hg_out_gain, mla_q_a_gain, mla_w_uq, mla_kv_a_gain, mla_w_ukv,
           mla_q_gain, mla_k_gain, mla_out_gain, w_out, norm2_gain, w_ff1, w_ff2):
    batch, seq, _ = x.shape
    t = batch * seq
    depth = w_in.shape[0]
    gm_w = GM_HEADS * HEAD64
    hg_w = HG_HEADS * HEAD64
    n_main = 2 * gm_w + 4 * hg_w + Q_LORA + KV_LORA

    w_in_p = jnp.zeros((depth, D_MODEL, D_IN_PAD), F32)
    w_in_p = w_in_p.at[:, :, :n_main].set(w_in[:, :, :n_main])
    w_in_p = w_in_p.at[:, :, n_main + QK_NOPE:n_main + QK_DIM].set(w_in[:, :, n_main:])
    w_in_p = w_in_p.astype(BF16)

    wq = mla_w_uq.reshape(depth, Q_LORA, MLA_HEADS, QK_DIM)
    wq = jnp.pad(wq, ((0, 0), (0, 0), (0, 0), (0, LANES - QK_DIM)))
    wq = wq.reshape(depth, Q_LORA, MLA_PAD).astype(BF16)
    wkv = mla_w_ukv.reshape(depth, KV_LORA, MLA_HEADS, QK_NOPE + HEAD64)
    wk = jnp.pad(wkv[..., :QK_NOPE], ((0, 0), (0, 0), (0, 0), (0, LANES - QK_NOPE)))
    wk = wk.reshape(depth, KV_LORA, MLA_PAD).astype(BF16)
    wv_heads = wkv[..., QK_NOPE:]
    wv = jnp.zeros((depth, KV_LORA, MLA_HEADS, LANES), F32)
    wv = wv.at[:, :, 0::2, :HEAD64].set(wv_heads[:, :, 0::2])
    wv = wv.at[:, :, 1::2, HEAD64:].set(wv_heads[:, :, 1::2])
    wv = wv.reshape(depth, KV_LORA, MLA_PAD).astype(BF16)
    q_gain_p = jnp.pad(mla_q_gain, ((0, 0), (0, LANES - QK_DIM)))
    k_gain_p = jnp.pad(mla_k_gain, ((0, 0), (0, LANES - QK_DIM)))

    w_out_b = w_out.astype(BF16)
    w1_b = w_ff1.astype(BF16)
    w2_b = w_ff2.astype(BF16)
    bias_full = jnp.repeat(jnp.swapaxes(gm_b_s, 1, 2), HEAD64, axis=2)

    half = QK_ROPE // 2
    inv_freq = ROPE_THETA ** (-jnp.arange(half, dtype=F32) / half)
    invf_lane = jnp.zeros((1, LANES), F32)
    invf_lane = invf_lane.at[0, QK_NOPE:QK_NOPE + half].set(inv_freq)
    invf_lane = invf_lane.at[0, QK_NOPE + half:QK_DIM].set(inv_freq)
    sign_lane = jnp.zeros((1, LANES), F32)
    sign_lane = sign_lane.at[0, QK_NOPE:QK_NOPE + half].set(-1.0)
    sign_lane = sign_lane.at[0, QK_NOPE + half:QK_DIM].set(1.0)
    cos_t, sin_t = _rope_tables(positions.reshape(t, 1), invf_lane, sign_lane)

    xs = x.reshape(t, D_MODEL)
    for l in range(depth):
        proj = _in_proj(xs, norm1_gain[l][None, :], w_in_p[l])
        y_a = _gm_mixer(proj, gm_v_gain[l][None, :], gm_w_s[l], bias_full[l],
                        gm_out_gain[l][None, :])
        y_b = _hg_mixer(proj, hg_lower_bound, hg_out_gain[l][None, :], l, batch, seq)
        q, k, v = _mla_prep(proj, cos_t, sin_t, mla_q_a_gain[l][None, :],
                            mla_kv_a_gain[l][None, :], wq[l], wk[l], wv[l],
                            q_gain_p[l][None, :], k_gain_p[l][None, :])
        y_c = _mla_attn(q, k, v, mla_out_gain[l][None, :], batch, seq)
        xs = _out_ffn(xs, y_a, y_b, y_c, w_out_b[l, :gm_w], w_out_b[l, gm_w:gm_w + hg_w],
                      w_out_b[l, gm_w + hg_w:], norm2_gain[l][None, :], w1_b[l], w2_b[l])
    return xs.reshape(batch, seq, D_MODEL)
```

```python
import functools

import jax
import jax.numpy as jnp
from jax import lax
from jax.experimental import pallas as pl
from jax.experimental.pallas import tpu as pltpu

F32 = jnp.float32
BF16 = jnp.bfloat16

D_MODEL = 1024
DEPTH = 4
CHUNK = 128
EPS = 1e-6
GM_HEADS = 4
HG_HEADS = 4
HEAD64 = 64
PAIR = 2 * HEAD64
MLA_HEADS = 8
QK_NOPE = 64
QK_ROPE = 32
QK_DIM = QK_NOPE + QK_ROPE
Q_LORA = 256
KV_LORA = 128
MLA_WIDTH = MLA_HEADS * HEAD64
ROPE_THETA = 10000.0
LOG2_E = 1.4426950408889634
D_FF = 4 * D_MODEL
LANES = 128
D_IN_PAD = 2048
MLA_PAD = MLA_HEADS * LANES

VMEM_LIMIT = 48 * 1024 * 1024

TM_IN = 512
TM_GM = 512
TM_MLA = 512
TQ = 256
TM_FFN = 512
TF_FFN = 1024


def _params(*sem):
    return pltpu.CompilerParams(dimension_semantics=sem, vmem_limit_bytes=VMEM_LIMIT)


def _dot(a, b):
    return jnp.dot(a, b, preferred_element_type=F32)


def _dot_nt(a, b):
    return lax.dot_general(a, b, (((1,), (1,)), ((), ())), preferred_element_type=F32)


def _dot_tn(a, b):
    return lax.dot_general(a, b, (((0,), (0,)), ((), ())), preferred_element_type=F32)


def _split_dot(x, w_bf16, parts, w_on_left=False):
    acc = None
    rem = x
    for _ in range(parts):
        piece = rem.astype(BF16)
        term = _dot(w_bf16, piece) if w_on_left else _dot(piece, w_bf16)
        acc = term if acc is None else acc + term
        rem = rem - piece.astype(F32)
    return acc


def _head_mean_matrix(width):
    r = lax.broadcasted_iota(jnp.int32, (width, width), 0) // HEAD64
    c = lax.broadcasted_iota(jnp.int32, (width, width), 1) // HEAD64
    return jnp.where(r == c, 1.0 / HEAD64, 0.0).astype(BF16)


def _head_rms(x, gmat):
    ms = _split_dot(x * x, gmat, 2)
    return x * lax.rsqrt(ms + EPS)


def _in_proj_kernel(x_ref, g_ref, w_ref, o_ref):
    x = x_ref[...]
    ms = jnp.mean(x * x, axis=-1, keepdims=True)
    h = x * lax.rsqrt(ms + EPS) * g_ref[...]
    o_ref[...] = _dot(h.astype(BF16), w_ref[...])


def _in_proj(x, gain, w_stack, layer):
    t = x.shape[0]
    return pl.pallas_call(
        _in_proj_kernel,
        grid=(t // TM_IN,),
        in_specs=[
            pl.BlockSpec((TM_IN, D_MODEL), lambda i: (i, 0)),
            pl.BlockSpec((1, D_MODEL), lambda i: (0, 0)),
            pl.BlockSpec((None, D_MODEL, D_IN_PAD), lambda i: (layer, 0, 0)),
        ],
        out_specs=pl.BlockSpec((TM_IN, D_IN_PAD), lambda i: (i, 0)),
        out_shape=jax.ShapeDtypeStruct((t, D_IN_PAD), F32),
        compiler_params=_params("parallel"),
        name="in_proj",
    )(x, gain, w_stack)


def _gm_kernel(u_ref, v_ref, vg_ref, w_ref, b_ref, og_ref, o_ref):
    width = GM_HEADS * HEAD64
    gmat = _head_mean_matrix(width)
    u = jax.nn.gelu(u_ref[...])
    v = _head_rms(jax.nn.gelu(v_ref[...]), gmat) * vg_ref[...]
    row = lax.broadcasted_iota(jnp.int32, (CHUNK, CHUNK), 0)
    col = lax.broadcasted_iota(jnp.int32, (CHUNK, CHUNK), 1)
    causal = row >= col
    lane_head = lax.broadcasted_iota(jnp.int32, (1, PAIR), 1) // HEAD64
    w_tril = [jnp.where(causal, w_ref[h], 0.0).astype(BF16) for h in range(GM_HEADS)]
    bias = b_ref[...]
    for c in range(TM_GM // CHUNK):
        rows = slice(c * CHUNK, (c + 1) * CHUNK)
        y_blocks = []
        for p in range(GM_HEADS // 2):
            vb = v[rows, p * PAIR:(p + 1) * PAIR]
            acc = None
            for hh in range(2):
                vm = jnp.where(lane_head == hh, vb, 0.0).astype(BF16)
                term = _dot(w_tril[2 * p + hh], vm)
                acc = term if acc is None else acc + term
            y_blocks.append(acc)
        y = jnp.concatenate(y_blocks, axis=-1) + bias
        out = u[rows] * y
        out = _head_rms(out, gmat) * og_ref[...]
        o_ref[rows, :] = out.astype(o_ref.dtype)


def _gm_mixer(proj, v_gain, w_s, bias_full, out_gain):
    t = proj.shape[0]
    width = GM_HEADS * HEAD64
    return pl.pallas_call(
        _gm_kernel,
        grid=(t // TM_GM,),
        in_specs=[
            pl.BlockSpec((TM_GM, width), lambda i: (i, 0)),
            pl.BlockSpec((TM_GM, width), lambda i: (i, 1)),
            pl.BlockSpec((1, width), lambda i: (0, 0)),
            pl.BlockSpec((GM_HEADS, CHUNK, CHUNK), lambda i: (0, 0, 0)),
            pl.BlockSpec((CHUNK, width), lambda i: (0, 0)),
            pl.BlockSpec((1, width), lambda i: (0, 0)),
        ],
        out_specs=pl.BlockSpec((TM_GM, width), lambda i: (i, 0)),
        out_shape=jax.ShapeDtypeStruct((t, width), BF16),
        compiler_params=_params("parallel"),
        name="gm_mixer",
    )(proj, proj, v_gain, w_s, bias_full, out_gain)


def _anchor_rows(b, m):
    n = b.shape[0]
    if m >= 8:
        pieces = []
        for blk in range(n // (2 * m)):
            a = blk * 2 * m + m - 1
            pieces.append(jnp.broadcast_to(b[a:a + 1, :], (2 * m, b.shape[1])))
        return pieces[0] if len(pieces) == 1 else jnp.concatenate(pieces, axis=0)
    pos = lax.broadcasted_iota(jnp.int32, (n, 1), 0) % (2 * m)
    r = b
    for p in range(2 * m):
        shift = p - (m - 1)
        if shift == 0:
            continue
        r = jnp.where(pos == p, pltpu.roll(b, shift % n, axis=0), r)
    return r


def _hg_kernel(layer, q_ref, f_ref, i_ref, g_ref, lb_ref, og_ref, o_ref, st_ref):
    width = HG_HEADS * HEAD64
    n_pairs = HG_HEADS // 2

    @pl.when(pl.program_id(1) == 0)
    def _():
        st_ref[...] = jnp.zeros_like(st_ref)

    lbs = lb_ref[...]
    e = jnp.exp(lbs - jnp.max(lbs, axis=0, keepdims=True))
    soft = e / jnp.sum(e, axis=0, keepdims=True)
    lb = jnp.zeros((1, width), F32)
    for i in range(1, layer + 1):
        lb = lb + soft[i:i + 1, :]

    qr = q_ref[...]
    q = qr * jax.nn.sigmoid(qr)
    f = lb + (1.0 - lb) * jax.nn.sigmoid(f_ref[...])
    kk = 1.0 - f
    lf = jnp.log(f)
    v = i_ref[...]

    row = lax.broadcasted_iota(jnp.int32, (CHUNK, CHUNK), 0)
    col = lax.broadcasted_iota(jnp.int32, (CHUNK, CHUNK), 1)
    tri = jnp.where(row >= col, 1.0, 0.0).astype(BF16)
    b = _split_dot(lf, tri, 3, w_on_left=True)

    lane_head = lax.broadcasted_iota(jnp.int32, (1, PAIR), 1) // HEAD64
    t_idx = lax.broadcasted_iota(jnp.int32, (CHUNK, 1), 0)

    def pair(x, p):
        return x[:, p * PAIR:(p + 1) * PAIR]

    def head_only(xb, hh):
        return jnp.where(lane_head == hh, xb, 0.0).astype(BF16)

    scores = [jnp.zeros((CHUNK, CHUNK), F32) for _ in range(HG_HEADS)]
    m = CHUNK // 2
    while m >= 1:
        upper = (t_idx % (2 * m)) >= m
        r = _anchor_rows(b, m)
        w = jnp.exp(jnp.minimum(jnp.where(upper, b - r, r - b), 0.0))
        u = jnp.where(upper, q, kk) * w
        mask = ((row // (2 * m)) == (col // (2 * m))) & ((row % (2 * m)) >= m) & ((col % (2 * m)) < m)
        for p in range(n_pairs):
            ub = pair(u, p)
            ubf = ub.astype(BF16)
            for hh in range(2):
                s = _dot_nt(ubf, head_only(ub, hh))
                scores[2 * p + hh] = scores[2 * p + hh] + jnp.where(mask, s, 0.0)
        m //= 2
    diag = row == col
    for p in range(n_pairs):
        qb = pair(q, p).astype(BF16)
        for hh in range(2):
            s = _dot_nt(qb, head_only(pair(kk, p), hh))
            scores[2 * p + hh] = scores[2 * p + hh] + jnp.where(diag, s, 0.0)

    qd = q * jnp.exp(b)
    b_last = b[CHUNK - 1:CHUNK, :]
    kd = kk * jnp.exp(jnp.minimum(b_last - b, 0.0))
    state_decay = jnp.exp(b_last)
    rr = lax.broadcasted_iota(jnp.int32, (PAIR, PAIR), 0) // HEAD64
    cc = lax.broadcasted_iota(jnp.int32, (PAIR, PAIR), 1) // HEAD64
    same_head = rr == cc

    o_blocks = []
    for p in range(n_pairs):
        st = st_ref[p]
        vb = pair(v, p)
        o = _dot_nt(pair(qd, p).astype(BF16), st.astype(BF16))
        for hh in range(2):
            o = o + _dot(scores[2 * p + hh].astype(BF16), head_only(vb, hh))
        o_blocks.append(o)
        upd = _dot_tn(vb.astype(BF16), pair(kd, p).astype(BF16))
        st_ref[p] = st * pair(state_decay, p) + jnp.where(same_head, upd, 0.0)
    o = jnp.concatenate(o_blocks, axis=-1)

    gmat = _head_mean_matrix(width)
    gr = g_ref[...]
    out = _head_rms(o, gmat) * og_ref[...] * (gr * jax.nn.sigmoid(gr))
    o_ref[...] = out.astype(o_ref.dtype)


def _hg_mixer(proj, lower_bound_raw, out_gain, layer, batch, seq):
    t = proj.shape[0]
    width = HG_HEADS * HEAD64
    nc = seq // CHUNK

    def col_block(j):
        return pl.BlockSpec((CHUNK, width), lambda bi, ci: (bi * nc + ci, j))

    return pl.pallas_call(
        functools.partial(_hg_kernel, layer),
        grid=(batch, nc),
        in_specs=[
            col_block(2), col_block(3), col_block(4), col_block(5),
            pl.BlockSpec((DEPTH, width), lambda bi, ci: (0, 0)),
            pl.BlockSpec((1, width), lambda bi, ci: (0, 0)),
        ],
        out_specs=pl.BlockSpec((CHUNK, width), lambda bi, ci: (bi * nc + ci, 0)),
        out_shape=jax.ShapeDtypeStruct((t, width), BF16),
        scratch_shapes=[pltpu.VMEM((HG_HEADS // 2, PAIR, PAIR), F32)],
        compiler_params=_params("parallel", "arbitrary"),
        name="hg_mixer",
    )(proj, proj, proj, proj, lower_bound_raw, out_gain)


def _rope_table_kernel(pos_ref, invf_ref, sign_ref, cos_ref, sin_ref):
    ang = pos_ref[...].astype(F32) * invf_ref[...]
    cos_ref[...] = jnp.cos(ang)
    sin_ref[...] = jnp.sin(ang) * sign_ref[...]


def _rope_tables(pos_col, invf_lane, sign_lane):
    t = pos_col.shape[0]
    tm = 2048
    return pl.pallas_call(
        _rope_table_kernel,
        grid=(t // tm,),
        in_specs=[
            pl.BlockSpec((tm, 1), lambda i: (i, 0)),
            pl.BlockSpec((1, LANES), lambda i: (0, 0)),
            pl.BlockSpec((1, LANES), lambda i: (0, 0)),
        ],
        out_specs=[pl.BlockSpec((tm, LANES), lambda i: (i, 0))] * 2,
        out_shape=[jax.ShapeDtypeStruct((t, LANES), F32)] * 2,
        compiler_params=_params("parallel"),
        name="rope_tables",
    )(pos_col, invf_lane, sign_lane)


def _swap_rope_halves(x):
    lane = lax.broadcasted_iota(jnp.int32, (1, LANES), 1)
    half = QK_ROPE // 2
    from_above = pltpu.roll(x, LANES - half, axis=1)
    from_below = pltpu.roll(x, half, axis=1)
    return jnp.where(lane < QK_NOPE + half, from_above, from_below)


def _mla_prep_kernel(cq_ref, ckv_ref, kpe_ref, cos_ref, sin_ref, qag_ref, kvag_ref,
                     wq_ref, wk_ref, wv_ref, qg_ref, kg_ref, q_out, k_out, vt_out):
    def rms(x, gain):
        ms = jnp.mean(x * x, axis=-1, keepdims=True)
        return x * lax.rsqrt(ms + EPS) * gain

    hq = rms(cq_ref[...], qag_ref[...]).astype(BF16)
    hkv = rms(ckv_ref[...], kvag_ref[...]).astype(BF16)
    q = _dot(hq, wq_ref[...])
    kn = _dot(hkv, wk_ref[...])
    free = (lax.broadcasted_iota(jnp.int32, (1, MLA_PAD), 1) // HEAD64) % 4
    ones = jnp.where((free == 1) | (free == 2), 1.0, 0.0)
    vt_out[...] = (_dot(hkv, wv_ref[...]) + ones).T.astype(vt_out.dtype)
    kpe = kpe_ref[...]
    cos = cos_ref[...]
    sin = sin_ref[...]
    scale = QK_DIM ** -0.5 * LOG2_E

    def norm_rope(x, gain):
        ms = jnp.sum(x * x, axis=-1, keepdims=True) * (1.0 / QK_DIM)
        xn = x * lax.rsqrt(ms + EPS) * gain
        return xn * cos + _swap_rope_halves(xn) * sin

    for h in range(MLA_HEADS):
        lanes = slice(h * LANES, (h + 1) * LANES)
        q_out[:, lanes] = (norm_rope(q[:, lanes], qg_ref[...]) * scale).astype(q_out.dtype)
        k_out[:, lanes] = norm_rope(kn[:, lanes] + kpe, kg_ref[...]).astype(k_out.dtype)


def _mla_prep(proj, cos_t, sin_t, qa_gain, kva_gain, wq, wk, wv, q_gain, k_gain, batch, seq):
    t = proj.shape[0]
    tiles_per_seq = seq // TM_MLA
    full = lambda shape: pl.BlockSpec(shape, lambda i: (0,) * len(shape))
    out = jax.ShapeDtypeStruct((t, MLA_PAD), BF16)
    out_t = jax.ShapeDtypeStruct((batch, MLA_PAD, seq), BF16)
    return pl.pallas_call(
        _mla_prep_kernel,
        grid=(t // TM_MLA,),
        in_specs=[
            pl.BlockSpec((TM_MLA, Q_LORA), lambda i: (i, 6)),
            pl.BlockSpec((TM_MLA, KV_LORA), lambda i: (i, 14)),
            pl.BlockSpec((TM_MLA, LANES), lambda i: (i, 15)),
            pl.BlockSpec((TM_MLA, LANES), lambda i: (i, 0)),
            pl.BlockSpec((TM_MLA, LANES), lambda i: (i, 0)),
            full((1, Q_LORA)), full((1, KV_LORA)),
            full((Q_LORA, MLA_PAD)), full((KV_LORA, MLA_PAD)), full((KV_LORA, MLA_PAD)),
            full((1, LANES)), full((1, LANES)),
        ],
        out_specs=[pl.BlockSpec((TM_MLA, MLA_PAD), lambda i: (i, 0))] * 2 + [
            pl.BlockSpec((None, MLA_PAD, TM_MLA),
                         lambda i: (i // tiles_per_seq, 0, i % tiles_per_seq))],
        out_shape=[out, out, out_t],
        compiler_params=_params("parallel"),
        name="mla_prep",
    )(proj, proj, proj, cos_t, sin_t, qa_gain, kva_gain, wq, wk, wv, q_gain, k_gain)


def _mla_attn_kernel(q_ref, k_ref, vt_ref, og_ref, o_ref, st_ref, acc_ref):
    qi = pl.program_id(1)
    row = lax.broadcasted_iota(jnp.int32, (TQ, TQ), 0)
    col = lax.broadcasted_iota(jnp.int32, (TQ, TQ), 1)
    causal_t = col >= row

    def scores(j, maxes, masked):
        start = pl.multiple_of(j * TQ, TQ)
        new = []
        for h in range(MLA_HEADS):
            lanes = slice(h * LANES, (h + 1) * LANES)
            st = _dot_nt(k_ref[pl.ds(start, TQ), lanes], q_ref[:, lanes])
            if masked:
                st = jnp.where(causal_t, st, -jnp.inf)
            st_ref[h, pl.ds(start, TQ), :] = st
            new.append(jnp.maximum(maxes[h], jnp.max(st, axis=0, keepdims=True)))
        return tuple(new)

    init = tuple(jnp.full((1, TQ), -jnp.inf, F32) for _ in range(MLA_HEADS))
    maxes = lax.fori_loop(0, qi, lambda j, mx: scores(j, mx, False), init)
    maxes = scores(qi, maxes, True)

    acc_ref[...] = jnp.zeros(acc_ref.shape, F32)

    def weighted_values(j, carry):
        start = pl.multiple_of(j * TQ, TQ)
        for h in range(MLA_HEADS):
            lanes = slice(h * LANES, (h + 1) * LANES)
            pt = jnp.exp2(st_ref[h, pl.ds(start, TQ), :] - maxes[h]).astype(BF16)
            acc_ref[h] += _dot(vt_ref[lanes, pl.ds(start, TQ)], pt)
        return carry

    lax.fori_loop(0, qi + 1, weighted_values, 0)

    low = lax.broadcasted_iota(jnp.int32, (LANES, 1), 0) < HEAD64
    pairs = []
    for p in range(MLA_HEADS // 2):
        acc_e = acc_ref[2 * p]
        acc_o = acc_ref[2 * p + 1]
        ot = jnp.where(low, acc_e / acc_e[HEAD64:HEAD64 + 1, :], acc_o / acc_o[0:1, :])
        sq = ot * ot
        ms_e = jnp.sum(sq[:HEAD64], axis=0, keepdims=True) * (1.0 / HEAD64)
        ms_o = jnp.sum(sq[HEAD64:], axis=0, keepdims=True) * (1.0 / HEAD64)
        ot = ot * lax.rsqrt(jnp.where(low, ms_e, ms_o) + EPS)
        pairs.append(ot.T)
    o_ref[...] = (jnp.concatenate(pairs, axis=-1) * og_ref[...]).astype(o_ref.dtype)


def _mla_attn(q, k, vt, out_gain, batch, seq):
    t = q.shape[0]
    nq = seq // TQ
    return pl.pallas_call(
        _mla_attn_kernel,
        grid=(batch, nq),
        in_specs=[
            pl.BlockSpec((TQ, MLA_PAD), lambda bi, qi: (bi * nq + qi, 0)),
            pl.BlockSpec((seq, MLA_PAD), lambda bi, qi: (bi, 0)),
            pl.BlockSpec((None, MLA_PAD, seq), lambda bi, qi: (bi, 0, 0)),
            pl.BlockSpec((1, MLA_WIDTH), lambda bi, qi: (0, 0)),
        ],
        out_specs=pl.BlockSpec((TQ, MLA_WIDTH), lambda bi, qi: (bi * nq + qi, 0)),
        out_shape=jax.ShapeDtypeStruct((t, MLA_WIDTH), BF16),
        scratch_shapes=[pltpu.VMEM((MLA_HEADS, seq, TQ), F32),
                        pltpu.VMEM((MLA_HEADS, LANES, TQ), F32)],
        compiler_params=_params("parallel", "arbitrary"),
        name="mla_attn",
    )(q, k, vt, out_gain)


def _out_ffn_kernel(x_ref, ya_ref, yb_ref, yc_ref, woa_ref, wob_ref, woc_ref, g_ref,
                    w1_ref, w2_ref, o_ref, h_ref):
    j = pl.program_id(1)

    @pl.when(j == 0)
    def _():
        xn = (x_ref[...] + _dot(ya_ref[...], woa_ref[...]) + _dot(yb_ref[...], wob_ref[...])
              + _dot(yc_ref[...], woc_ref[...]))
        ms = jnp.mean(xn * xn, axis=-1, keepdims=True)
        h_ref[...] = (xn * lax.rsqrt(ms + EPS) * g_ref[...]).astype(h_ref.dtype)
        o_ref[...] = xn

    a = jnp.maximum(_dot(h_ref[...], w1_ref[...]), 0.0)
    o_ref[...] += _dot((a * a).astype(BF16), w2_ref[...])


def _out_ffn(x, ya, yb, yc, w_out_stack, gain2, w1_stack, w2_stack, layer):
    t = x.shape[0]
    wa, wb, wc = ya.shape[1], yb.shape[1], yc.shape[1]
    assert wa == wb and wc == 2 * wa
    row = lambda width: pl.BlockSpec((TM_FFN, width), lambda i, j: (i, 0))
    wo_rows = lambda width, blk: pl.BlockSpec((None, width, D_MODEL), lambda i, j: (layer, blk, 0))
    return pl.pallas_call(
        _out_ffn_kernel,
        grid=(t // TM_FFN, D_FF // TF_FFN),
        in_specs=[
            row(D_MODEL), row(wa), row(wb), row(wc),
            wo_rows(wa, 0), wo_rows(wb, 1), wo_rows(wc, 1),
            pl.BlockSpec((1, D_MODEL), lambda i, j: (0, 0)),
            pl.BlockSpec((None, D_MODEL, TF_FFN), lambda i, j: (layer, 0, j)),
            pl.BlockSpec((None, TF_FFN, D_MODEL), lambda i, j: (layer, j, 0)),
        ],
        out_specs=row(D_MODEL),
        out_shape=jax.ShapeDtypeStruct((t, D_MODEL), F32),
        scratch_shapes=[pltpu.VMEM((TM_FFN, D_MODEL), BF16)],
        compiler_params=_params("parallel", "arbitrary"),
        name="out_ffn",
    )(x, ya, yb, yc, w_out_stack, w_out_stack, w_out_stack, gain2, w1_stack, w2_stack)


def kernel(x, positions, norm1_gain, w_in, gm_v_gain, gm_w_s, gm_b_s, gm_out_gain,
           hg_lower_bound, hg_out_gain, mla_q_a_gain, mla_w_uq, mla_kv_a_gain, mla_w_ukv,
           mla_q_gain, mla_k_gain, mla_out_gain, w_out, norm2_gain, w_ff1, w_ff2):
    batch, seq, _ = x.shape
    t = batch * seq
    depth = w_in.shape[0]
    gm_w = GM_HEADS * HEAD64
    hg_w = HG_HEADS * HEAD64
    n_main = 2 * gm_w + 4 * hg_w + Q_LORA + KV_LORA

    w_in_b = w_in.astype(BF16)
    w_in_p = jnp.concatenate(
        [w_in_b[:, :, :n_main],
         jnp.zeros((depth, D_MODEL, QK_NOPE), BF16),
         w_in_b[:, :, n_main:],
         jnp.zeros((depth, D_MODEL, LANES - QK_DIM), BF16)], axis=-1)

    wq = mla_w_uq.astype(BF16).reshape(depth, Q_LORA, MLA_HEADS, QK_DIM)
    wq = jnp.pad(wq, ((0, 0), (0, 0), (0, 0), (0, LANES - QK_DIM)))
    wq = wq.reshape(depth, Q_LORA, MLA_PAD)
    wkv = mla_w_ukv.astype(BF16).reshape(depth, KV_LORA, MLA_HEADS // 2, 2, QK_NOPE + HEAD64)
    wk = jnp.pad(wkv[..., :QK_NOPE], ((0, 0),) * 4 + ((0, LANES - QK_NOPE),))
    wk = wk.reshape(depth, KV_LORA, MLA_PAD)
    zeros_v = jnp.zeros((depth, KV_LORA, MLA_HEADS // 2, HEAD64), BF16)
    wv = jnp.concatenate([wkv[:, :, :, 0, QK_NOPE:], zeros_v, zeros_v, wkv[:, :, :, 1, QK_NOPE:]],
                         axis=-1).reshape(depth, KV_LORA, MLA_PAD)
    q_gain_p = jnp.pad(mla_q_gain, ((0, 0), (0, LANES - QK_DIM)))
    k_gain_p = jnp.pad(mla_k_gain, ((0, 0), (0, LANES - QK_DIM)))

    w_out_b = w_out.astype(BF16)
    w1_b = w_ff1.astype(BF16)
    w2_b = w_ff2.astype(BF16)
    bias_full = jnp.broadcast_to(jnp.swapaxes(gm_b_s, 1, 2)[..., None],
                                 (depth, CHUNK, GM_HEADS, HEAD64)).reshape(depth, CHUNK, gm_w)

    half = QK_ROPE // 2
    inv_freq = ROPE_THETA ** (-jnp.arange(half, dtype=F32) / half)
    invf_lane = jnp.zeros((1, LANES), F32)
    invf_lane = invf_lane.at[0, QK_NOPE:QK_NOPE + half].set(inv_freq)
    invf_lane = invf_lane.at[0, QK_NOPE + half:QK_DIM].set(inv_freq)
    sign_lane = jnp.zeros((1, LANES), F32)
    sign_lane = sign_lane.at[0, QK_NOPE:QK_NOPE + half].set(-1.0)
    sign_lane = sign_lane.at[0, QK_NOPE + half:QK_DIM].set(1.0)
    cos_t, sin_t = _rope_tables(positions.reshape(t, 1), invf_lane, sign_lane)

    xs = x.reshape(t, D_MODEL)
    for l in range(depth):
        proj = _in_proj(xs, norm1_gain[l][None, :], w_in_p, l)
        y_a = _gm_mixer(proj, gm_v_gain[l][None, :], gm_w_s[l], bias_full[l],
                        gm_out_gain[l][None, :])
        y_b = _hg_mixer(proj, hg_lower_bound, hg_out_gain[l][None, :], l, batch, seq)
        q, k, v = _mla_prep(proj, cos_t, sin_t, mla_q_a_gain[l][None, :],
                            mla_kv_a_gain[l][None, :], wq[l], wk[l], wv[l],
                            q_gain_p[l][None, :], k_gain_p[l][None, :], batch, seq)
        y_c = _mla_attn(q, k, v, mla_out_gain[l][None, :], batch, seq)
        xs = _out_ffn(xs, y_a, y_b, y_c, w_out_b, norm2_gain[l][None, :], w1_b, w2_b, l)
    return xs.reshape(batch, seq, D_MODEL)
```

```python
import functools

import jax
import jax.numpy as jnp
from jax import lax
from jax.experimental import pallas as pl
from jax.experimental.pallas import tpu as pltpu

F32 = jnp.float32
BF16 = jnp.bfloat16

D_MODEL = 1024
DEPTH = 4
CHUNK = 128
EPS = 1e-6
GM_HEADS = 4
HG_HEADS = 4
HEAD64 = 64
PAIR = 2 * HEAD64
MLA_HEADS = 8
QK_NOPE = 64
QK_ROPE = 32
QK_DIM = QK_NOPE + QK_ROPE
Q_LORA = 256
KV_LORA = 128
MLA_WIDTH = MLA_HEADS * HEAD64
ROPE_THETA = 10000.0
LOG2_E = 1.4426950408889634
D_FF = 4 * D_MODEL
LANES = 128
D_IN_PAD = 2048
MLA_PAD = MLA_HEADS * LANES

VMEM_LIMIT = 56 * 1024 * 1024

TM_IN = 512
TM_GM = 512
HG_CHUNKS_PER_STEP = 4
TM_MLA = 512
TQ = 256
TM_FFN = 1024
TF_FFN = 1024


def _params(*sem):
    return pltpu.CompilerParams(dimension_semantics=sem, vmem_limit_bytes=VMEM_LIMIT)


def _dot(a, b):
    return jnp.dot(a, b, preferred_element_type=F32)


def _dot_nt(a, b):
    return lax.dot_general(a, b, (((1,), (1,)), ((), ())), preferred_element_type=F32)


def _dot_tn(a, b):
    return lax.dot_general(a, b, (((0,), (0,)), ((), ())), preferred_element_type=F32)


def _split_dot(x, w_bf16, parts, w_on_left=False):
    acc = None
    rem = x
    for _ in range(parts):
        piece = rem.astype(BF16)
        term = _dot(w_bf16, piece) if w_on_left else _dot(piece, w_bf16)
        acc = term if acc is None else acc + term
        rem = rem - piece.astype(F32)
    return acc


def _head_mean_matrix(width):
    r = lax.broadcasted_iota(jnp.int32, (width, width), 0) // HEAD64
    c = lax.broadcasted_iota(jnp.int32, (width, width), 1) // HEAD64
    return jnp.where(r == c, 1.0 / HEAD64, 0.0).astype(BF16)


def _head_rms(x, gmat):
    ms = _split_dot(x * x, gmat, 2)
    return x * lax.rsqrt(ms + EPS)


def _in_proj_kernel(x_ref, g_ref, w_ref, o_ref):
    x = x_ref[...]
    ms = jnp.mean(x * x, axis=-1, keepdims=True)
    h = x * lax.rsqrt(ms + EPS) * g_ref[...]
    o_ref[...] = _dot(h.astype(BF16), w_ref[...])


def _in_proj(x, gain, w_stack, layer):
    t = x.shape[0]
    return pl.pallas_call(
        _in_proj_kernel,
        grid=(t // TM_IN,),
        in_specs=[
            pl.BlockSpec((TM_IN, D_MODEL), lambda i: (i, 0)),
            pl.BlockSpec((1, D_MODEL), lambda i: (0, 0)),
            pl.BlockSpec((None, D_MODEL, D_IN_PAD), lambda i: (layer, 0, 0)),
        ],
        out_specs=pl.BlockSpec((TM_IN, D_IN_PAD), lambda i: (i, 0)),
        out_shape=jax.ShapeDtypeStruct((t, D_IN_PAD), F32),
        compiler_params=_params("parallel"),
        name="in_proj",
    )(x, gain, w_stack)


def _gm_kernel(u_ref, v_ref, vg_ref, w_ref, b_ref, og_ref, o_ref):
    width = GM_HEADS * HEAD64
    gmat = _head_mean_matrix(width)
    u = jax.nn.gelu(u_ref[...])
    v = _head_rms(jax.nn.gelu(v_ref[...]), gmat) * vg_ref[...]
    row = lax.broadcasted_iota(jnp.int32, (CHUNK, CHUNK), 0)
    col = lax.broadcasted_iota(jnp.int32, (CHUNK, CHUNK), 1)
    causal = row >= col
    lane_head = lax.broadcasted_iota(jnp.int32, (1, PAIR), 1) // HEAD64
    w_tril = [jnp.where(causal, w_ref[h], 0.0).astype(BF16) for h in range(GM_HEADS)]
    bias = b_ref[...]
    for c in range(TM_GM // CHUNK):
        rows = slice(c * CHUNK, (c + 1) * CHUNK)
        y_blocks = []
        for p in range(GM_HEADS // 2):
            vb = v[rows, p * PAIR:(p + 1) * PAIR]
            acc = None
            for hh in range(2):
                vm = jnp.where(lane_head == hh, vb, 0.0).astype(BF16)
                term = _dot(w_tril[2 * p + hh], vm)
                acc = term if acc is None else acc + term
            y_blocks.append(acc)
        y = jnp.concatenate(y_blocks, axis=-1) + bias
        out = u[rows] * y
        out = _head_rms(out, gmat) * og_ref[...]
        o_ref[rows, :] = out.astype(o_ref.dtype)


def _gm_mixer(proj, v_gain, w_s, bias_full, out_gain):
    t = proj.shape[0]
    width = GM_HEADS * HEAD64
    return pl.pallas_call(
        _gm_kernel,
        grid=(t // TM_GM,),
        in_specs=[
            pl.BlockSpec((TM_GM, width), lambda i: (i, 0)),
            pl.BlockSpec((TM_GM, width), lambda i: (i, 1)),
            pl.BlockSpec((1, width), lambda i: (0, 0)),
            pl.BlockSpec((GM_HEADS, CHUNK, CHUNK), lambda i: (0, 0, 0)),
            pl.BlockSpec((CHUNK, width), lambda i: (0, 0)),
            pl.BlockSpec((1, width), lambda i: (0, 0)),
        ],
        out_specs=pl.BlockSpec((TM_GM, width), lambda i: (i, 0)),
        out_shape=jax.ShapeDtypeStruct((t, width), BF16),
        compiler_params=_params("parallel"),
        name="gm_mixer",
    )(proj, proj, v_gain, w_s, bias_full, out_gain)


def _anchor_rows(b, m):
    n = b.shape[0]
    if m >= 8:
        pieces = []
        for blk in range(n // (2 * m)):
            a = blk * 2 * m + m - 1
            pieces.append(jnp.broadcast_to(b[a:a + 1, :], (2 * m, b.shape[1])))
        return pieces[0] if len(pieces) == 1 else jnp.concatenate(pieces, axis=0)
    b3 = b.reshape(n // 8, 8, b.shape[1])
    sub = lax.broadcasted_iota(jnp.int32, (1, 8, 1), 1)
    r = None
    for blk in range(8 // (2 * m)):
        a = blk * 2 * m + m - 1
        cand = jnp.broadcast_to(b3[:, a:a + 1, :], b3.shape)
        r = cand if r is None else jnp.where(sub >= blk * 2 * m, cand, r)
    return r.reshape(n, b.shape[1])


def _hg_kernel(layer, q_ref, f_ref, i_ref, g_ref, lb_ref, og_ref, o_ref, st_ref):
    width = HG_HEADS * HEAD64
    n_pairs = HG_HEADS // 2

    @pl.when(pl.program_id(1) == 0)
    def _():
        st_ref[...] = jnp.zeros_like(st_ref)

    lbs = lb_ref[...]
    e = jnp.exp(lbs - jnp.max(lbs, axis=0, keepdims=True))
    soft = e / jnp.sum(e, axis=0, keepdims=True)
    lb = jnp.zeros((1, width), F32)
    for i in range(1, layer + 1):
        lb = lb + soft[i:i + 1, :]

    qr = q_ref[...]
    q_all = qr * jax.nn.sigmoid(qr)
    f_all = lb + (1.0 - lb) * jax.nn.sigmoid(f_ref[...])
    kk_all = 1.0 - f_all
    lf_all = jnp.log(f_all)
    v_all = i_ref[...]

    row = lax.broadcasted_iota(jnp.int32, (CHUNK, CHUNK), 0)
    col = lax.broadcasted_iota(jnp.int32, (CHUNK, CHUNK), 1)
    tri = jnp.where(row >= col, 1.0, 0.0).astype(BF16)
    lane_head = lax.broadcasted_iota(jnp.int32, (1, PAIR), 1) // HEAD64
    t_idx = lax.broadcasted_iota(jnp.int32, (CHUNK, 1), 0)
    rr = lax.broadcasted_iota(jnp.int32, (PAIR, PAIR), 0) // HEAD64
    cc = lax.broadcasted_iota(jnp.int32, (PAIR, PAIR), 1) // HEAD64
    same_head = rr == cc
    levels = []
    m = CHUNK // 2
    while m >= 1:
        levels.append((m, (t_idx % (2 * m)) >= m,
                       ((row // (2 * m)) == (col // (2 * m))) & ((row % (2 * m)) >= m)
                       & ((col % (2 * m)) < m)))
        m //= 2
    diag = row == col

    def pair(x, p):
        return x[:, p * PAIR:(p + 1) * PAIR]

    def head_only(xb, hh):
        return jnp.where(lane_head == hh, xb, 0.0).astype(BF16)

    def one_chunk(q, kk, f, lf, v):
        b = _split_dot(lf, tri, 3, w_on_left=True)
        scores = [jnp.zeros((CHUNK, CHUNK), F32) for _ in range(HG_HEADS)]
        for m, upper, mask in levels:
            if m == 1:
                w = jnp.where(upper, f, 1.0)
            else:
                r = _anchor_rows(b, m)
                w = jnp.exp(jnp.minimum(jnp.where(upper, b - r, r - b), 0.0))
            u = jnp.where(upper, q, kk) * w
            for p in range(n_pairs):
                ub = pair(u, p)
                ubf = ub.astype(BF16)
                for hh in range(2):
                    s = _dot_nt(ubf, head_only(ub, hh))
                    scores[2 * p + hh] = jnp.where(mask, s, scores[2 * p + hh])
        for p in range(n_pairs):
            qb = pair(q, p).astype(BF16)
            for hh in range(2):
                s = _dot_nt(qb, head_only(pair(kk, p), hh))
                scores[2 * p + hh] = jnp.where(diag, s, scores[2 * p + hh])

        qd = q * jnp.exp(b)
        b_last = b[CHUNK - 1:CHUNK, :]
        kd = kk * jnp.exp(jnp.minimum(b_last - b, 0.0))
        state_decay = jnp.exp(b_last)
        o_blocks = []
        for p in range(n_pairs):
            st = st_ref[p]
            vb = pair(v, p)
            o = _dot_nt(pair(qd, p).astype(BF16), st.astype(BF16))
            for hh in range(2):
                o = o + _dot(scores[2 * p + hh].astype(BF16), head_only(vb, hh))
            o_blocks.append(o)
            upd = _dot_tn(vb.astype(BF16), pair(kd, p).astype(BF16))
            st_ref[p] = st * pair(state_decay, p) + jnp.where(same_head, upd, 0.0)
        return jnp.concatenate(o_blocks, axis=-1)

    outs = []
    for c in range(HG_CHUNKS_PER_STEP):
        rows = slice(c * CHUNK, (c + 1) * CHUNK)
        outs.append(one_chunk(q_all[rows], kk_all[rows], f_all[rows], lf_all[rows], v_all[rows]))
    o = jnp.concatenate(outs, axis=0)

    gmat = _head_mean_matrix(width)
    gr = g_ref[...]
    out = _head_rms(o, gmat) * og_ref[...] * (gr * jax.nn.sigmoid(gr))
    o_ref[...] = out.astype(o_ref.dtype)


def _hg_mixer(proj, lower_bound_raw, out_gain, layer, batch, seq):
    t = proj.shape[0]
    width = HG_HEADS * HEAD64
    rows = HG_CHUNKS_PER_STEP * CHUNK
    nc = seq // rows

    def col_block(j):
        return pl.BlockSpec((rows, width), lambda bi, ci: (bi * nc + ci, j))

    return pl.pallas_call(
        functools.partial(_hg_kernel, layer),
        grid=(batch, nc),
        in_specs=[
            col_block(2), col_block(3), col_block(4), col_block(5),
            pl.BlockSpec((DEPTH, width), lambda bi, ci: (0, 0)),
            pl.BlockSpec((1, width), lambda bi, ci: (0, 0)),
        ],
        out_specs=pl.BlockSpec((rows, width), lambda bi, ci: (bi * nc + ci, 0)),
        out_shape=jax.ShapeDtypeStruct((t, width), BF16),
        scratch_shapes=[pltpu.VMEM((HG_HEADS // 2, PAIR, PAIR), F32)],
        compiler_params=_params("parallel", "arbitrary"),
        name="hg_mixer",
    )(proj, proj, proj, proj, lower_bound_raw, out_gain)


def _rope_table_kernel(pos_col_ref, pos_row_ref, invf_lane_ref, sign_ref, invf_col_ref,
                       cos_ref, sin_ref, cos_t_ref, sin_t_ref):
    ang = pos_col_ref[...].astype(F32) * invf_lane_ref[...]
    cos_ref[...] = jnp.cos(ang)
    sin_ref[...] = jnp.sin(ang) * sign_ref[...]
    ang_t = invf_col_ref[...] * pos_row_ref[...].astype(F32)
    cos_t_ref[...] = jnp.cos(ang_t)
    sin_t_ref[...] = jnp.sin(ang_t)


def _rope_tables(pos_col, pos_row, invf_lane, sign_lane, invf_col):
    t = pos_col.shape[0]
    half = QK_ROPE // 2
    tm = 2048
    return pl.pallas_call(
        _rope_table_kernel,
        grid=(t // tm,),
        in_specs=[
            pl.BlockSpec((tm, 1), lambda i: (i, 0)),
            pl.BlockSpec((1, tm), lambda i: (0, i)),
            pl.BlockSpec((1, LANES), lambda i: (0, 0)),
            pl.BlockSpec((1, LANES), lambda i: (0, 0)),
            pl.BlockSpec((half, 1), lambda i: (0, 0)),
        ],
        out_specs=[pl.BlockSpec((tm, LANES), lambda i: (i, 0))] * 2
        + [pl.BlockSpec((half, tm), lambda i: (0, i))] * 2,
        out_shape=[jax.ShapeDtypeStruct((t, LANES), F32)] * 2
        + [jax.ShapeDtypeStruct((half, t), F32)] * 2,
        compiler_params=_params("parallel"),
        name="rope_tables",
    )(pos_col, pos_row, invf_lane, sign_lane, invf_col)


def _swap_rope_halves(x):
    lane = lax.broadcasted_iota(jnp.int32, (1, LANES), 1)
    half = QK_ROPE // 2
    from_above = pltpu.roll(x, LANES - half, axis=1)
    from_below = pltpu.roll(x, half, axis=1)
    return jnp.where(lane < QK_NOPE + half, from_above, from_below)


def _mla_prep_kernel(cq_ref, ckv_ref, kpe_ref, cos_ref, sin_ref, cos_t_ref, sin_t_ref,
                     qag_ref, kvag_ref, wqt_ref, wk_ref, wvt_ref, qg_ref, kg_ref, kgs_ref,
                     qt_out, k_out, vt_out):
    def rms(x, gain):
        ms = jnp.mean(x * x, axis=-1, keepdims=True)
        return x * lax.rsqrt(ms + EPS) * gain

    half = QK_ROPE // 2
    hq = rms(cq_ref[...], qag_ref[...]).astype(BF16)
    hkv = rms(ckv_ref[...], kvag_ref[...]).astype(BF16)

    qt = _dot_nt(wqt_ref[...], hq)
    cos_t = cos_t_ref[...]
    sin_t = sin_t_ref[...]
    for h in range(MLA_HEADS):
        x = qt[h * QK_DIM:(h + 1) * QK_DIM, :]
        ms = jnp.sum(x * x, axis=0, keepdims=True) * (1.0 / QK_DIM)
        xn = x * lax.rsqrt(ms + EPS) * qg_ref[...]
        r1 = xn[QK_NOPE:QK_NOPE + half]
        r2 = xn[QK_NOPE + half:]
        base = h * LANES
        qt_out[base:base + QK_NOPE, :] = xn[:QK_NOPE].astype(qt_out.dtype)
        qt_out[base + QK_NOPE:base + QK_NOPE + half, :] = (r1 * cos_t - r2 * sin_t).astype(qt_out.dtype)
        qt_out[base + QK_NOPE + half:base + QK_DIM, :] = (r2 * cos_t + r1 * sin_t).astype(qt_out.dtype)
        qt_out[base + QK_DIM:base + LANES, :] = jnp.zeros((LANES - QK_DIM, x.shape[1]), qt_out.dtype)

    vt = _dot_nt(wvt_ref[...], hkv).astype(vt_out.dtype)
    ones = jnp.ones((HEAD64, vt.shape[1]), vt_out.dtype)
    for h in range(MLA_HEADS):
        vals = vt[h * HEAD64:(h + 1) * HEAD64, :]
        lo, hi = (vals, ones) if h % 2 == 0 else (ones, vals)
        vt_out[h * LANES:h * LANES + HEAD64, :] = lo
        vt_out[h * LANES + HEAD64:(h + 1) * LANES, :] = hi

    kn = _dot(hkv, wk_ref[...])
    kpe = kpe_ref[...]
    gain_cos = kg_ref[...] * cos_ref[...]
    rot = _swap_rope_halves(kpe) * kgs_ref[...] * sin_ref[...]
    for h in range(MLA_HEADS):
        lanes = slice(h * LANES, (h + 1) * LANES)
        x = kn[:, lanes] + kpe
        ms = jnp.sum(x * x, axis=-1, keepdims=True) * (1.0 / QK_DIM)
        k_out[:, lanes] = (lax.rsqrt(ms + EPS) * (x * gain_cos + rot)).astype(k_out.dtype)


def _mla_prep(proj, tables, qa_gain, kva_gain, wqt, wk, wvt, q_gain_b, k_gain, k_gain_sw,
              batch, seq):
    t = proj.shape[0]
    half = QK_ROPE // 2
    tiles_per_seq = seq // TM_MLA
    cos_n, sin_n, cos_t, sin_t = tables
    full = lambda shape: pl.BlockSpec(shape, lambda i: (0,) * len(shape))
    out = jax.ShapeDtypeStruct((t, MLA_PAD), BF16)
    out_t = jax.ShapeDtypeStruct((batch, MLA_PAD, seq), BF16)
    spec_t = pl.BlockSpec((None, MLA_PAD, TM_MLA),
                          lambda i: (i // tiles_per_seq, 0, i % tiles_per_seq))
    return pl.pallas_call(
        _mla_prep_kernel,
        grid=(t // TM_MLA,),
        in_specs=[
            pl.BlockSpec((TM_MLA, Q_LORA), lambda i: (i, 6)),
            pl.BlockSpec((TM_MLA, KV_LORA), lambda i: (i, 14)),
            pl.BlockSpec((TM_MLA, LANES), lambda i: (i, 15)),
            pl.BlockSpec((TM_MLA, LANES), lambda i: (i, 0)),
            pl.BlockSpec((TM_MLA, LANES), lambda i: (i, 0)),
            pl.BlockSpec((half, TM_MLA), lambda i: (0, i)),
            pl.BlockSpec((half, TM_MLA), lambda i: (0, i)),
            full((1, Q_LORA)), full((1, KV_LORA)),
            full((MLA_HEADS * QK_DIM, Q_LORA)), full((KV_LORA, MLA_PAD)),
            full((MLA_WIDTH, KV_LORA)),
            full((QK_DIM, TM_MLA)), full((1, LANES)), full((1, LANES)),
        ],
        out_specs=[spec_t, pl.BlockSpec((TM_MLA, MLA_PAD), lambda i: (i, 0)), spec_t],
        out_shape=[out_t, out, out_t],
        compiler_params=_params("parallel"),
        name="mla_prep",
    )(proj, proj, proj, cos_n, sin_n, cos_t, sin_t, qa_gain, kva_gain, wqt, wk, wvt,
      q_gain_b, k_gain, k_gain_sw)


def _mla_attn_kernel(qt_ref, k_ref, vt_ref, og_ref, o_ref, st_ref, acc_ref, m_ref):
    qi = pl.program_id(1)
    odd = qi % 2 == 1
    even = qi % 2 == 0

    def causal(rows):
        row = lax.broadcasted_iota(jnp.int32, (rows, TQ), 0)
        col = lax.broadcasted_iota(jnp.int32, (rows, TQ), 1)
        return col >= row - (rows - TQ)

    def scores(start, rows, maxes, mask):
        new = []
        for h in range(MLA_HEADS):
            lanes = slice(h * LANES, (h + 1) * LANES)
            st = _dot(k_ref[pl.ds(start, rows), lanes], qt_ref[lanes, :])
            if mask is not None:
                st = jnp.where(mask, st, -jnp.inf)
            st_ref[h, pl.ds(start, rows), :] = st
            new.append(jnp.maximum(maxes[h], jnp.max(st, axis=0, keepdims=True)))
        return tuple(new)

    def full_pair(i, maxes):
        return scores(pl.multiple_of(i * 2 * TQ, 2 * TQ), 2 * TQ, maxes, None)

    init = tuple(jnp.full((1, TQ), -jnp.inf, F32) for _ in range(MLA_HEADS))
    maxes = lax.fori_loop(0, qi // 2, full_pair, init)
    for h in range(MLA_HEADS):
        m_ref[h] = jnp.broadcast_to(maxes[h], m_ref.shape[1:])

    def tail_scores(start, rows, mask):
        prev = tuple(m_ref[h, 0:1, :] for h in range(MLA_HEADS))
        new = scores(start, rows, prev, mask)
        for h in range(MLA_HEADS):
            m_ref[h] = jnp.broadcast_to(new[h], m_ref.shape[1:])

    @pl.when(odd)
    def _():
        tail_scores(pl.multiple_of((qi - 1) * TQ, TQ), 2 * TQ, causal(2 * TQ))

    @pl.when(even)
    def _():
        tail_scores(pl.multiple_of(qi * TQ, TQ), TQ, causal(TQ))

    acc_ref[...] = jnp.zeros(acc_ref.shape, F32)

    def weighted_values(start, rows):
        for h in range(MLA_HEADS):
            lanes = slice(h * LANES, (h + 1) * LANES)
            x = st_ref[h, pl.ds(start, rows), :] - m_ref[h, 0:1, :]
            pt = jnp.exp2(x.astype(BF16))
            acc_ref[h] += _dot(vt_ref[lanes, pl.ds(start, rows)], pt)

    def value_pair(i, carry):
        weighted_values(pl.multiple_of(i * 2 * TQ, 2 * TQ), 2 * TQ)
        return carry

    lax.fori_loop(0, (qi + 1) // 2, value_pair, 0)

    @pl.when(even)
    def _():
        weighted_values(pl.multiple_of(qi * TQ, TQ), TQ)

    low = lax.broadcasted_iota(jnp.int32, (LANES, 1), 0) < HEAD64
    pairs = []
    for p in range(MLA_HEADS // 2):
        acc_e = acc_ref[2 * p]
        acc_o = acc_ref[2 * p + 1]
        ot = jnp.where(low, acc_e / acc_e[HEAD64:HEAD64 + 1, :], acc_o / acc_o[0:1, :])
        sq = ot * ot
        ms_e = jnp.sum(sq[:HEAD64], axis=0, keepdims=True) * (1.0 / HEAD64)
        ms_o = jnp.sum(sq[HEAD64:], axis=0, keepdims=True) * (1.0 / HEAD64)
        ot = ot * lax.rsqrt(jnp.where(low, ms_e, ms_o) + EPS)
        pairs.append(ot.T)
    o_ref[...] = (jnp.concatenate(pairs, axis=-1) * og_ref[...]).astype(o_ref.dtype)


def _mla_attn(qt, k, vt, out_gain, batch, seq):
    t = k.shape[0]
    nq = seq // TQ
    return pl.pallas_call(
        _mla_attn_kernel,
        grid=(batch, nq),
        in_specs=[
            pl.BlockSpec((None, MLA_PAD, TQ), lambda bi, qi: (bi, 0, qi)),
            pl.BlockSpec((seq, MLA_PAD), lambda bi, qi: (bi, 0)),
            pl.BlockSpec((None, MLA_PAD, seq), lambda bi, qi: (bi, 0, 0)),
            pl.BlockSpec((1, MLA_WIDTH), lambda bi, qi: (0, 0)),
        ],
        out_specs=pl.BlockSpec((TQ, MLA_WIDTH), lambda bi, qi: (bi * nq + qi, 0)),
        out_shape=jax.ShapeDtypeStruct((t, MLA_WIDTH), BF16),
        scratch_shapes=[pltpu.VMEM((MLA_HEADS, seq, TQ), F32),
                        pltpu.VMEM((MLA_HEADS, LANES, TQ), F32),
                        pltpu.VMEM((MLA_HEADS, 8, TQ), F32)],
        compiler_params=_params("parallel", "arbitrary"),
        name="mla_attn",
    )(qt, k, vt, out_gain)


def _out_ffn_kernel(x_ref, ya_ref, yb_ref, yc_ref, woa_ref, wob_ref, woc_ref, g_ref,
                    w1_ref, w2_ref, o_ref, h_ref):
    j = pl.program_id(1)

    @pl.when(j == 0)
    def _():
        xn = (x_ref[...] + _dot(ya_ref[...], woa_ref[...]) + _dot(yb_ref[...], wob_ref[...])
              + _dot(yc_ref[...], woc_ref[...]))
        ms = jnp.mean(xn * xn, axis=-1, keepdims=True)
        h_ref[...] = (xn * lax.rsqrt(ms + EPS) * g_ref[...]).astype(h_ref.dtype)
        o_ref[...] = xn

    a = jnp.maximum(_dot(h_ref[...], w1_ref[...]), 0.0)
    o_ref[...] += _dot((a * a).astype(BF16), w2_ref[...])


def _out_ffn(x, ya, yb, yc, w_out_stack, gain2, w1_stack, w2_stack, layer):
    t = x.shape[0]
    wa, wb, wc = ya.shape[1], yb.shape[1], yc.shape[1]
    assert wa == wb and wc == 2 * wa
    row = lambda width: pl.BlockSpec((TM_FFN, width), lambda i, j: (i, 0))
    wo_rows = lambda width, blk: pl.BlockSpec((None, width, D_MODEL), lambda i, j: (layer, blk, 0))
    return pl.pallas_call(
        _out_ffn_kernel,
        grid=(t // TM_FFN, D_FF // TF_FFN),
        in_specs=[
            row(D_MODEL), row(wa), row(wb), row(wc),
            wo_rows(wa, 0), wo_rows(wb, 1), wo_rows(wc, 1),
            pl.BlockSpec((1, D_MODEL), lambda i, j: (0, 0)),
            pl.BlockSpec((None, D_MODEL, TF_FFN), lambda i, j: (layer, 0, j)),
            pl.BlockSpec((None, TF_FFN, D_MODEL), lambda i, j: (layer, j, 0)),
        ],
        out_specs=row(D_MODEL),
        out_shape=jax.ShapeDtypeStruct((t, D_MODEL), F32),
        scratch_shapes=[pltpu.VMEM((TM_FFN, D_MODEL), BF16)],
        compiler_params=_params("parallel", "arbitrary"),
        name="out_ffn",
    )(x, ya, yb, yc, w_out_stack, w_out_stack, w_out_stack, gain2, w1_stack, w2_stack)


def kernel(x, positions, norm1_gain, w_in, gm_v_gain, gm_w_s, gm_b_s, gm_out_gain,
           hg_lower_bound, hg_out_gain, mla_q_a_gain, mla_w_uq, mla_kv_a_gain, mla_w_ukv,
           mla_q_gain, mla_k_gain, mla_out_gain, w_out, norm2_gain, w_ff1, w_ff2):
    batch, seq, _ = x.shape
    t = batch * seq
    depth = w_in.shape[0]
    gm_w = GM_HEADS * HEAD64
    hg_w = HG_HEADS * HEAD64
    n_main = 2 * gm_w + 4 * hg_w + Q_LORA + KV_LORA

    w_in_b = w_in.astype(BF16)
    w_in_p = jnp.concatenate(
        [w_in_b[:, :, :n_main],
         jnp.zeros((depth, D_MODEL, QK_NOPE), BF16),
         w_in_b[:, :, n_main:],
         jnp.zeros((depth, D_MODEL, LANES - QK_DIM), BF16)], axis=-1)

    half = QK_ROPE // 2
    wqt = jnp.swapaxes(mla_w_uq, 1, 2).astype(BF16)
    wkv = mla_w_ukv.astype(BF16).reshape(depth, KV_LORA, MLA_HEADS, QK_NOPE + HEAD64)
    wk = jnp.pad(wkv[..., :QK_NOPE], ((0, 0),) * 3 + ((0, LANES - QK_NOPE),))
    wk = wk.reshape(depth, KV_LORA, MLA_PAD)
    wvt = jnp.swapaxes(wkv[..., QK_NOPE:].reshape(depth, KV_LORA, MLA_WIDTH), 1, 2)
    q_gain_b = jnp.broadcast_to((mla_q_gain * (QK_DIM ** -0.5 * LOG2_E))[:, :, None],
                                (depth, QK_DIM, TM_MLA))
    pad_lanes = ((0, 0), (0, LANES - QK_DIM))
    k_gain_p = jnp.pad(mla_k_gain, pad_lanes)
    k_gain_sw = jnp.pad(jnp.concatenate(
        [mla_k_gain[:, :QK_NOPE], mla_k_gain[:, QK_NOPE + half:], mla_k_gain[:, QK_NOPE:QK_NOPE + half]],
        axis=-1), pad_lanes)

    w_out_b = w_out.astype(BF16)
    w1_b = w_ff1.astype(BF16)
    w2_b = w_ff2.astype(BF16)
    bias_full = jnp.broadcast_to(jnp.swapaxes(gm_b_s, 1, 2)[..., None],
                                 (depth, CHUNK, GM_HEADS, HEAD64)).reshape(depth, CHUNK, gm_w)

    inv_freq = ROPE_THETA ** (-jnp.arange(half, dtype=F32) / half)
    invf_lane = jnp.zeros((1, LANES), F32)
    invf_lane = invf_lane.at[0, QK_NOPE:QK_NOPE + half].set(inv_freq)
    invf_lane = invf_lane.at[0, QK_NOPE + half:QK_DIM].set(inv_freq)
    sign_lane = jnp.zeros((1, LANES), F32)
    sign_lane = sign_lane.at[0, QK_NOPE:QK_NOPE + half].set(-1.0)
    sign_lane = sign_lane.at[0, QK_NOPE + half:QK_DIM].set(1.0)
    tables = _rope_tables(positions.reshape(t, 1), positions.reshape(1, t), invf_lane, sign_lane,
                          inv_freq[:, None])

    xs = x.reshape(t, D_MODEL)
    for l in range(depth):
        proj = _in_proj(xs, norm1_gain[l][None, :], w_in_p, l)
        y_a = _gm_mixer(proj, gm_v_gain[l][None, :], gm_w_s[l], bias_full[l],
                        gm_out_gain[l][None, :])
        y_b = _hg_mixer(proj, hg_lower_bound, hg_out_gain[l][None, :], l, batch, seq)
        qt, k, vt = _mla_prep(proj, tables, mla_q_a_gain[l][None, :], mla_kv_a_gain[l][None, :],
                              wqt[l], wk[l], wvt[l], q_gain_b[l], k_gain_p[l][None, :],
                              k_gain_sw[l][None, :], batch, seq)
        y_c = _mla_attn(qt, k, vt, mla_out_gain[l][None, :], batch, seq)
        xs = _out_ffn(xs, y_a, y_b, y_c, w_out_b, norm2_gain[l][None, :], w1_b, w2_b, l)
    return xs.reshape(batch, seq, D_MODEL)
```

```python
import functools

import jax
import jax.numpy as jnp
from jax import lax
from jax.experimental import pallas as pl
from jax.experimental.pallas import tpu as pltpu

F32 = jnp.float32
BF16 = jnp.bfloat16

D_MODEL = 1024
DEPTH = 4
CHUNK = 128
EPS = 1e-6
GM_HEADS = 4
HG_HEADS = 4
HEAD64 = 64
PAIR = 2 * HEAD64
MLA_HEADS = 8
QK_NOPE = 64
QK_ROPE = 32
QK_DIM = QK_NOPE + QK_ROPE
Q_LORA = 256
KV_LORA = 128
MLA_WIDTH = MLA_HEADS * HEAD64
ROPE_THETA = 10000.0
LOG2_E = 1.4426950408889634
D_FF = 4 * D_MODEL
LANES = 128
D_IN_PAD = 2048
MLA_PAD = MLA_HEADS * LANES

VMEM_LIMIT = 56 * 1024 * 1024

TM_IN = 512
TM_GM = 512
HG_CHUNKS_PER_STEP = 4
TM_MLA = 512
TQ = 256
TM_FFN = 1024
TF_FFN = 1024


def _params(*sem):
    return pltpu.CompilerParams(dimension_semantics=sem, vmem_limit_bytes=VMEM_LIMIT)


def _dot(a, b):
    return jnp.dot(a, b, preferred_element_type=F32)


def _dot_nt(a, b):
    return lax.dot_general(a, b, (((1,), (1,)), ((), ())), preferred_element_type=F32)


def _dot_tn(a, b):
    return lax.dot_general(a, b, (((0,), (0,)), ((), ())), preferred_element_type=F32)


def _split_dot(x, w_bf16, parts, w_on_left=False):
    acc = None
    rem = x
    for _ in range(parts):
        piece = rem.astype(BF16)
        term = _dot(w_bf16, piece) if w_on_left else _dot(piece, w_bf16)
        acc = term if acc is None else acc + term
        rem = rem - piece.astype(F32)
    return acc


def _head_mean_matrix(width):
    r = lax.broadcasted_iota(jnp.int32, (width, width), 0) // HEAD64
    c = lax.broadcasted_iota(jnp.int32, (width, width), 1) // HEAD64
    return jnp.where(r == c, 1.0 / HEAD64, 0.0).astype(BF16)


def _head_rms(x, gmat):
    ms = _split_dot(x * x, gmat, 2)
    return x * lax.rsqrt(ms + EPS)


def _in_proj_kernel(x_ref, g_ref, w_ref, o_ref):
    x = x_ref[...]
    ms = jnp.mean(x * x, axis=-1, keepdims=True)
    h = x * lax.rsqrt(ms + EPS) * g_ref[...]
    o_ref[...] = _dot(h.astype(BF16), w_ref[...])


def _in_proj(x, gain, w_stack, layer):
    t = x.shape[0]
    return pl.pallas_call(
        _in_proj_kernel,
        grid=(t // TM_IN,),
        in_specs=[
            pl.BlockSpec((TM_IN, D_MODEL), lambda i: (i, 0)),
            pl.BlockSpec((1, D_MODEL), lambda i: (0, 0)),
            pl.BlockSpec((None, D_MODEL, D_IN_PAD), lambda i: (layer, 0, 0)),
        ],
        out_specs=pl.BlockSpec((TM_IN, D_IN_PAD), lambda i: (i, 0)),
        out_shape=jax.ShapeDtypeStruct((t, D_IN_PAD), F32),
        compiler_params=_params("parallel"),
        name="in_proj",
    )(x, gain, w_stack)


def _gm_kernel(u_ref, v_ref, vg_ref, w_ref, b_ref, og_ref, o_ref):
    width = GM_HEADS * HEAD64
    gmat = _head_mean_matrix(width)
    u = jax.nn.gelu(u_ref[...])
    v = _head_rms(jax.nn.gelu(v_ref[...]), gmat) * vg_ref[...]
    row = lax.broadcasted_iota(jnp.int32, (CHUNK, CHUNK), 0)
    col = lax.broadcasted_iota(jnp.int32, (CHUNK, CHUNK), 1)
    causal = row >= col
    lane_head = lax.broadcasted_iota(jnp.int32, (1, PAIR), 1) // HEAD64
    w_tril = [jnp.where(causal, w_ref[h], 0.0).astype(BF16) for h in range(GM_HEADS)]
    bias = b_ref[...]
    for c in range(TM_GM // CHUNK):
        rows = slice(c * CHUNK, (c + 1) * CHUNK)
        y_blocks = []
        for p in range(GM_HEADS // 2):
            vb = v[rows, p * PAIR:(p + 1) * PAIR]
            acc = None
            for hh in range(2):
                vm = jnp.where(lane_head == hh, vb, 0.0).astype(BF16)
                term = _dot(w_tril[2 * p + hh], vm)
                acc = term if acc is None else acc + term
            y_blocks.append(acc)
        y = jnp.concatenate(y_blocks, axis=-1) + bias
        out = u[rows] * y
        out = _head_rms(out, gmat) * og_ref[...]
        o_ref[rows, :] = out.astype(o_ref.dtype)


def _gm_mixer(proj, v_gain, w_s, bias_full, out_gain):
    t = proj.shape[0]
    width = GM_HEADS * HEAD64
    return pl.pallas_call(
        _gm_kernel,
        grid=(t // TM_GM,),
        in_specs=[
            pl.BlockSpec((TM_GM, width), lambda i: (i, 0)),
            pl.BlockSpec((TM_GM, width), lambda i: (i, 1)),
            pl.BlockSpec((1, width), lambda i: (0, 0)),
            pl.BlockSpec((GM_HEADS, CHUNK, CHUNK), lambda i: (0, 0, 0)),
            pl.BlockSpec((CHUNK, width), lambda i: (0, 0)),
            pl.BlockSpec((1, width), lambda i: (0, 0)),
        ],
        out_specs=pl.BlockSpec((TM_GM, width), lambda i: (i, 0)),
        out_shape=jax.ShapeDtypeStruct((t, width), BF16),
        compiler_params=_params("parallel"),
        name="gm_mixer",
    )(proj, proj, v_gain, w_s, bias_full, out_gain)


def _anchor_rows(b, m):
    n = b.shape[0]
    if m >= 8:
        pieces = []
        for blk in range(n // (2 * m)):
            a = blk * 2 * m + m - 1
            pieces.append(jnp.broadcast_to(b[a:a + 1, :], (2 * m, b.shape[1])))
        return pieces[0] if len(pieces) == 1 else jnp.concatenate(pieces, axis=0)
    b3 = b.reshape(n // 8, 8, b.shape[1])
    sub = lax.broadcasted_iota(jnp.int32, (1, 8, 1), 1)
    r = None
    for blk in range(8 // (2 * m)):
        a = blk * 2 * m + m - 1
        cand = jnp.broadcast_to(b3[:, a:a + 1, :], b3.shape)
        r = cand if r is None else jnp.where(sub >= blk * 2 * m, cand, r)
    return r.reshape(n, b.shape[1])


def _hg_kernel(layer, q_ref, f_ref, i_ref, g_ref, lb_ref, og_ref, o_ref, st_ref):
    width = HG_HEADS * HEAD64
    n_pairs = HG_HEADS // 2

    @pl.when(pl.program_id(1) == 0)
    def _():
        st_ref[...] = jnp.zeros_like(st_ref)

    lbs = lb_ref[...]
    e = jnp.exp(lbs - jnp.max(lbs, axis=0, keepdims=True))
    soft = e / jnp.sum(e, axis=0, keepdims=True)
    lb = jnp.zeros((1, width), F32)
    for i in range(1, layer + 1):
        lb = lb + soft[i:i + 1, :]

    qr = q_ref[...]
    q_all = qr * jax.nn.sigmoid(qr)
    f_all = lb + (1.0 - lb) * jax.nn.sigmoid(f_ref[...])
    kk_all = 1.0 - f_all
    lf_all = jnp.log(f_all)
    v_all = i_ref[...]

    row = lax.broadcasted_iota(jnp.int32, (CHUNK, CHUNK), 0)
    col = lax.broadcasted_iota(jnp.int32, (CHUNK, CHUNK), 1)
    tri = jnp.where(row >= col, 1.0, 0.0).astype(BF16)
    lane_head = lax.broadcasted_iota(jnp.int32, (1, PAIR), 1) // HEAD64
    t_idx = lax.broadcasted_iota(jnp.int32, (CHUNK, 1), 0)
    rr = lax.broadcasted_iota(jnp.int32, (PAIR, PAIR), 0) // HEAD64
    cc = lax.broadcasted_iota(jnp.int32, (PAIR, PAIR), 1) // HEAD64
    same_head = rr == cc
    levels = []
    m = CHUNK // 2
    while m >= 1:
        levels.append((m, (t_idx % (2 * m)) >= m,
                       ((row // (2 * m)) == (col // (2 * m))) & ((row % (2 * m)) >= m)
                       & ((col % (2 * m)) < m)))
        m //= 2
    diag = row == col

    def pair(x, p):
        return x[:, p * PAIR:(p + 1) * PAIR]

    def head_only(xb, hh):
        return jnp.where(lane_head == hh, xb, 0.0).astype(BF16)

    def one_chunk(q, kk, f, lf, v):
        b = _split_dot(lf, tri, 3, w_on_left=True)
        scores = [jnp.zeros((CHUNK, CHUNK), F32) for _ in range(HG_HEADS)]
        for m, upper, mask in levels:
            if m == 1:
                w = jnp.where(upper, f, 1.0)
            else:
                r = _anchor_rows(b, m)
                w = jnp.exp(jnp.minimum(jnp.where(upper, b - r, r - b), 0.0))
            u = jnp.where(upper, q, kk) * w
            for p in range(n_pairs):
                ub = pair(u, p)
                ubf = ub.astype(BF16)
                for hh in range(2):
                    s = _dot_nt(ubf, head_only(ub, hh))
                    scores[2 * p + hh] = jnp.where(mask, s, scores[2 * p + hh])
        for p in range(n_pairs):
            qb = pair(q, p).astype(BF16)
            for hh in range(2):
                s = _dot_nt(qb, head_only(pair(kk, p), hh))
                scores[2 * p + hh] = jnp.where(diag, s, scores[2 * p + hh])

        qd = q * jnp.exp(b)
        b_last = b[CHUNK - 1:CHUNK, :]
        kd = kk * jnp.exp(jnp.minimum(b_last - b, 0.0))
        state_decay = jnp.exp(b_last)
        o_blocks = []
        for p in range(n_pairs):
            st = st_ref[p]
            vb = pair(v, p)
            o = _dot_nt(pair(qd, p).astype(BF16), st.astype(BF16))
            for hh in range(2):
                o = o + _dot(scores[2 * p + hh].astype(BF16), head_only(vb, hh))
            o_blocks.append(o)
            upd = _dot_tn(vb.astype(BF16), pair(kd, p).astype(BF16))
            st_ref[p] = st * pair(state_decay, p) + jnp.where(same_head, upd, 0.0)
        return jnp.concatenate(o_blocks, axis=-1)

    outs = []
    for c in range(HG_CHUNKS_PER_STEP):
        rows = slice(c * CHUNK, (c + 1) * CHUNK)
        outs.append(one_chunk(q_all[rows], kk_all[rows], f_all[rows], lf_all[rows], v_all[rows]))
    o = jnp.concatenate(outs, axis=0)

    gmat = _head_mean_matrix(width)
    gr = g_ref[...]
    out = _head_rms(o, gmat) * og_ref[...] * (gr * jax.nn.sigmoid(gr))
    o_ref[...] = out.astype(o_ref.dtype)


def _hg_mixer(proj, lower_bound_raw, out_gain, layer, batch, seq):
    t = proj.shape[0]
    width = HG_HEADS * HEAD64
    rows = HG_CHUNKS_PER_STEP * CHUNK
    nc = seq // rows

    def col_block(j):
        return pl.BlockSpec((rows, width), lambda bi, ci: (bi * nc + ci, j))

    return pl.pallas_call(
        functools.partial(_hg_kernel, layer),
        grid=(batch, nc),
        in_specs=[
            col_block(2), col_block(3), col_block(4), col_block(5),
            pl.BlockSpec((DEPTH, width), lambda bi, ci: (0, 0)),
            pl.BlockSpec((1, width), lambda bi, ci: (0, 0)),
        ],
        out_specs=pl.BlockSpec((rows, width), lambda bi, ci: (bi * nc + ci, 0)),
        out_shape=jax.ShapeDtypeStruct((t, width), BF16),
        scratch_shapes=[pltpu.VMEM((HG_HEADS // 2, PAIR, PAIR), F32)],
        compiler_params=_params("parallel", "arbitrary"),
        name="hg_mixer",
    )(proj, proj, proj, proj, lower_bound_raw, out_gain)


def _rope_table_kernel(pos_col_ref, pos_row_ref, invf_lane_ref, sign_ref, invf_col_ref,
                       cos_ref, sin_ref, cos_t_ref, sin_t_ref):
    ang = pos_col_ref[...].astype(F32) * invf_lane_ref[...]
    cos_ref[...] = jnp.cos(ang)
    sin_ref[...] = jnp.sin(ang) * sign_ref[...]
    ang_t = invf_col_ref[...] * pos_row_ref[...].astype(F32)
    cos_t_ref[...] = jnp.cos(ang_t)
    sin_t_ref[...] = jnp.sin(ang_t)


def _rope_tables(pos_col, pos_row, invf_lane, sign_lane, invf_col):
    t = pos_col.shape[0]
    half = QK_ROPE // 2
    tm = 2048
    return pl.pallas_call(
        _rope_table_kernel,
        grid=(t // tm,),
        in_specs=[
            pl.BlockSpec((tm, 1), lambda i: (i, 0)),
            pl.BlockSpec((1, tm), lambda i: (0, i)),
            pl.BlockSpec((1, LANES), lambda i: (0, 0)),
            pl.BlockSpec((1, LANES), lambda i: (0, 0)),
            pl.BlockSpec((half, 1), lambda i: (0, 0)),
        ],
        out_specs=[pl.BlockSpec((tm, LANES), lambda i: (i, 0))] * 2
        + [pl.BlockSpec((half, tm), lambda i: (0, i))] * 2,
        out_shape=[jax.ShapeDtypeStruct((t, LANES), F32)] * 2
        + [jax.ShapeDtypeStruct((half, t), F32)] * 2,
        compiler_params=_params("parallel"),
        name="rope_tables",
    )(pos_col, pos_row, invf_lane, sign_lane, invf_col)


def _swap_rope_halves(x):
    lane = lax.broadcasted_iota(jnp.int32, (1, LANES), 1)
    half = QK_ROPE // 2
    from_above = pltpu.roll(x, LANES - half, axis=1)
    from_below = pltpu.roll(x, half, axis=1)
    return jnp.where(lane < QK_NOPE + half, from_above, from_below)


def _mla_prep_kernel(cq_ref, ckv_ref, kpe_ref, cos_ref, sin_ref, cos_t_ref, sin_t_ref,
                     qag_ref, kvag_ref, wqt_ref, wk_ref, wvt_ref, qg_ref, kg_ref, kgs_ref,
                     qt_out, k_out, vt_out):
    def rms(x, gain):
        ms = jnp.mean(x * x, axis=-1, keepdims=True)
        return x * lax.rsqrt(ms + EPS) * gain

    half = QK_ROPE // 2
    hq = rms(cq_ref[...], qag_ref[...]).astype(BF16)
    hkv = rms(ckv_ref[...], kvag_ref[...]).astype(BF16)

    qt = _dot_nt(wqt_ref[...], hq)
    cos_t = cos_t_ref[...]
    sin_t = sin_t_ref[...]
    for h in range(MLA_HEADS):
        x = qt[h * QK_DIM:(h + 1) * QK_DIM, :]
        ms = jnp.sum(x * x, axis=0, keepdims=True) * (1.0 / QK_DIM)
        xn = x * lax.rsqrt(ms + EPS) * qg_ref[...]
        r1 = xn[QK_NOPE:QK_NOPE + half]
        r2 = xn[QK_NOPE + half:]
        base = h * LANES
        qt_out[base:base + QK_NOPE, :] = xn[:QK_NOPE].astype(qt_out.dtype)
        qt_out[base + QK_NOPE:base + QK_NOPE + half, :] = (r1 * cos_t - r2 * sin_t).astype(qt_out.dtype)
        qt_out[base + QK_NOPE + half:base + QK_DIM, :] = (r2 * cos_t + r1 * sin_t).astype(qt_out.dtype)
        qt_out[base + QK_DIM:base + LANES, :] = jnp.zeros((LANES - QK_DIM, x.shape[1]), qt_out.dtype)

    vt = _dot_nt(wvt_ref[...], hkv).astype(vt_out.dtype)
    ones = jnp.ones((HEAD64, vt.shape[1]), vt_out.dtype)
    for h in range(MLA_HEADS):
        vals = vt[h * HEAD64:(h + 1) * HEAD64, :]
        lo, hi = (vals, ones) if h % 2 == 0 else (ones, vals)
        vt_out[h * LANES:h * LANES + HEAD64, :] = lo
        vt_out[h * LANES + HEAD64:(h + 1) * LANES, :] = hi

    kn = _dot(hkv, wk_ref[...])
    kpe = pltpu.roll(kpe_ref[...], QK_NOPE, axis=1)
    gain_cos = kg_ref[...] * cos_ref[...]
    rot = _swap_rope_halves(kpe) * kgs_ref[...] * sin_ref[...]
    for h in range(MLA_HEADS):
        lanes = slice(h * LANES, (h + 1) * LANES)
        x = kn[:, lanes] + kpe
        ms = jnp.sum(x * x, axis=-1, keepdims=True) * (1.0 / QK_DIM)
        k_out[:, lanes] = (lax.rsqrt(ms + EPS) * (x * gain_cos + rot)).astype(k_out.dtype)


def _mla_prep(proj, tables, qa_gain, kva_gain, wqt, wk, wvt, q_gain_b, k_gain, k_gain_sw,
              batch, seq):
    t = proj.shape[0]
    half = QK_ROPE // 2
    tiles_per_seq = seq // TM_MLA
    cos_n, sin_n, cos_t, sin_t = tables
    full = lambda shape: pl.BlockSpec(shape, lambda i: (0,) * len(shape))
    out = jax.ShapeDtypeStruct((t, MLA_PAD), BF16)
    out_t = jax.ShapeDtypeStruct((batch, MLA_PAD, seq), BF16)
    spec_t = pl.BlockSpec((None, MLA_PAD, TM_MLA),
                          lambda i: (i // tiles_per_seq, 0, i % tiles_per_seq))
    return pl.pallas_call(
        _mla_prep_kernel,
        grid=(t // TM_MLA,),
        in_specs=[
            pl.BlockSpec((TM_MLA, Q_LORA), lambda i: (i, 6)),
            pl.BlockSpec((TM_MLA, KV_LORA), lambda i: (i, 14)),
            pl.BlockSpec((TM_MLA, LANES), lambda i: (i, 15)),
            pl.BlockSpec((TM_MLA, LANES), lambda i: (i, 0)),
            pl.BlockSpec((TM_MLA, LANES), lambda i: (i, 0)),
            pl.BlockSpec((half, TM_MLA), lambda i: (0, i)),
            pl.BlockSpec((half, TM_MLA), lambda i: (0, i)),
            full((1, Q_LORA)), full((1, KV_LORA)),
            full((MLA_HEADS * QK_DIM, Q_LORA)), full((KV_LORA, MLA_PAD)),
            full((MLA_WIDTH, KV_LORA)),
            full((QK_DIM, TM_MLA)), full((1, LANES)), full((1, LANES)),
        ],
        out_specs=[spec_t, pl.BlockSpec((TM_MLA, MLA_PAD), lambda i: (i, 0)), spec_t],
        out_shape=[out_t, out, out_t],
        compiler_params=_params("parallel"),
        name="mla_prep",
    )(proj, proj, proj, cos_n, sin_n, cos_t, sin_t, qa_gain, kva_gain, wqt, wk, wvt,
      q_gain_b, k_gain, k_gain_sw)


def _mla_attn_kernel(qt_ref, k_ref, vt_ref, og_ref, o_ref, st_ref, acc_ref):
    seq = k_ref.shape[0]
    row = lax.broadcasted_iota(jnp.int32, (TQ, TQ), 0)
    col = lax.broadcasted_iota(jnp.int32, (TQ, TQ), 1)
    causal_t = col >= row
    low = lax.broadcasted_iota(jnp.int32, (LANES, 1), 0) < HEAD64
    gain = og_ref[...]

    for qi in range(seq // TQ):
        cols = slice(qi * TQ, (qi + 1) * TQ)
        n_keys = (qi + 1) * TQ
        spans = [(s, min(2 * TQ, n_keys - s)) for s in range(0, n_keys, 2 * TQ)]
        maxes = [None] * MLA_HEADS
        for start, rows in spans:
            for h in range(MLA_HEADS):
                lanes = slice(h * LANES, (h + 1) * LANES)
                st = _dot(k_ref[start:start + rows, lanes], qt_ref[lanes, cols])
                if start + rows == n_keys:
                    last = jnp.where(causal_t, st[rows - TQ:], -jnp.inf)
                    st = last if rows == TQ else jnp.concatenate([st[:rows - TQ], last], axis=0)
                st_ref[h, start:start + rows, :] = st
                cm = jnp.max(st, axis=0, keepdims=True)
                maxes[h] = cm if maxes[h] is None else jnp.maximum(maxes[h], cm)
        for i, (start, rows) in enumerate(spans):
            for h in range(MLA_HEADS):
                lanes = slice(h * LANES, (h + 1) * LANES)
                x = st_ref[h, start:start + rows, :] - maxes[h]
                pt = jnp.exp2(x.astype(BF16))
                part = _dot(vt_ref[lanes, start:start + rows], pt)
                if i == 0:
                    acc_ref[h] = part
                else:
                    acc_ref[h] += part
        pairs = []
        for p in range(MLA_HEADS // 2):
            acc_e = acc_ref[2 * p]
            acc_o = acc_ref[2 * p + 1]
            ot = jnp.where(low, acc_e / acc_e[HEAD64:HEAD64 + 1, :], acc_o / acc_o[0:1, :])
            sq = ot * ot
            ms_e = jnp.sum(sq[:HEAD64], axis=0, keepdims=True) * (1.0 / HEAD64)
            ms_o = jnp.sum(sq[HEAD64:], axis=0, keepdims=True) * (1.0 / HEAD64)
            ot = ot * lax.rsqrt(jnp.where(low, ms_e, ms_o) + EPS)
            pairs.append(ot.T)
        o_ref[cols, :] = (jnp.concatenate(pairs, axis=-1) * gain).astype(o_ref.dtype)


def _mla_attn(qt, k, vt, out_gain, batch, seq):
    t = k.shape[0]
    return pl.pallas_call(
        _mla_attn_kernel,
        grid=(batch,),
        in_specs=[
            pl.BlockSpec((None, MLA_PAD, seq), lambda bi: (bi, 0, 0)),
            pl.BlockSpec((seq, MLA_PAD), lambda bi: (bi, 0)),
            pl.BlockSpec((None, MLA_PAD, seq), lambda bi: (bi, 0, 0)),
            pl.BlockSpec((1, MLA_WIDTH), lambda bi: (0, 0)),
        ],
        out_specs=pl.BlockSpec((seq, MLA_WIDTH), lambda bi: (bi, 0)),
        out_shape=jax.ShapeDtypeStruct((t, MLA_WIDTH), BF16),
        scratch_shapes=[pltpu.VMEM((MLA_HEADS, seq, TQ), F32),
                        pltpu.VMEM((MLA_HEADS, LANES, TQ), F32)],
        compiler_params=_params("parallel"),
        name="mla_attn",
    )(qt, k, vt, out_gain)


def _out_ffn_kernel(x_ref, ya_ref, yb_ref, yc_ref, woa_ref, wob_ref, woc_ref, g_ref,
                    w1_ref, w2_ref, o_ref, h_ref):
    j = pl.program_id(1)

    @pl.when(j == 0)
    def _():
        xn = (x_ref[...] + _dot(ya_ref[...], woa_ref[...]) + _dot(yb_ref[...], wob_ref[...])
              + _dot(yc_ref[...], woc_ref[...]))
        ms = jnp.mean(xn * xn, axis=-1, keepdims=True)
        h_ref[...] = (xn * lax.rsqrt(ms + EPS) * g_ref[...]).astype(h_ref.dtype)
        o_ref[...] = xn

    a = jnp.maximum(_dot(h_ref[...], w1_ref[...]), 0.0)
    o_ref[...] += _dot((a * a).astype(BF16), w2_ref[...])


def _out_ffn(x, ya, yb, yc, w_out_stack, gain2, w1_stack, w2_stack, layer):
    t = x.shape[0]
    wa, wb, wc = ya.shape[1], yb.shape[1], yc.shape[1]
    assert wa == wb and wc == 2 * wa
    row = lambda width: pl.BlockSpec((TM_FFN, width), lambda i, j: (i, 0))
    wo_rows = lambda width, blk: pl.BlockSpec((None, width, D_MODEL), lambda i, j: (layer, blk, 0))
    return pl.pallas_call(
        _out_ffn_kernel,
        grid=(t // TM_FFN, D_FF // TF_FFN),
        in_specs=[
            row(D_MODEL), row(wa), row(wb), row(wc),
            wo_rows(wa, 0), wo_rows(wb, 1), wo_rows(wc, 1),
            pl.BlockSpec((1, D_MODEL), lambda i, j: (0, 0)),
            pl.BlockSpec((None, D_MODEL, TF_FFN), lambda i, j: (layer, 0, j)),
            pl.BlockSpec((None, TF_FFN, D_MODEL), lambda i, j: (layer, j, 0)),
        ],
        out_specs=row(D_MODEL),
        out_shape=jax.ShapeDtypeStruct((t, D_MODEL), F32),
        scratch_shapes=[pltpu.VMEM((TM_FFN, D_MODEL), BF16)],
        compiler_params=_params("parallel", "arbitrary"),
        name="out_ffn",
    )(x, ya, yb, yc, w_out_stack, w_out_stack, w_out_stack, gain2, w1_stack, w2_stack)


def kernel(x, positions, norm1_gain, w_in, gm_v_gain, gm_w_s, gm_b_s, gm_out_gain,
           hg_lower_bound, hg_out_gain, mla_q_a_gain, mla_w_uq, mla_kv_a_gain, mla_w_ukv,
           mla_q_gain, mla_k_gain, mla_out_gain, w_out, norm2_gain, w_ff1, w_ff2):
    batch, seq, _ = x.shape
    t = batch * seq
    depth = w_in.shape[0]
    gm_w = GM_HEADS * HEAD64

    w_in_p = jnp.pad(w_in.astype(BF16), ((0, 0), (0, 0), (0, D_IN_PAD - w_in.shape[2])))

    half = QK_ROPE // 2
    wqt = jnp.swapaxes(mla_w_uq, 1, 2).astype(BF16)
    wkv = mla_w_ukv.astype(BF16).reshape(depth, KV_LORA, MLA_HEADS, QK_NOPE + HEAD64)
    wk = jnp.pad(wkv[..., :QK_NOPE], ((0, 0),) * 3 + ((0, LANES - QK_NOPE),))
    wk = wk.reshape(depth, KV_LORA, MLA_PAD)
    wvt = jnp.swapaxes(wkv[..., QK_NOPE:].reshape(depth, KV_LORA, MLA_WIDTH), 1, 2)
    q_gain_b = jnp.broadcast_to((mla_q_gain * (QK_DIM ** -0.5 * LOG2_E))[:, :, None],
                                (depth, QK_DIM, TM_MLA))
    pad_lanes = ((0, 0), (0, LANES - QK_DIM))
    k_gain_p = jnp.pad(mla_k_gain, pad_lanes)
    k_gain_sw = jnp.pad(jnp.concatenate(
        [mla_k_gain[:, :QK_NOPE], mla_k_gain[:, QK_NOPE + half:], mla_k_gain[:, QK_NOPE:QK_NOPE + half]],
        axis=-1), pad_lanes)

    w_out_b = w_out.astype(BF16)
    w1_b = w_ff1.astype(BF16)
    w2_b = w_ff2.astype(BF16)
    bias_full = jnp.broadcast_to(jnp.swapaxes(gm_b_s, 1, 2)[..., None],
                                 (depth, CHUNK, GM_HEADS, HEAD64)).reshape(depth, CHUNK, gm_w)

    inv_freq = ROPE_THETA ** (-jnp.arange(half, dtype=F32) / half)
    invf_lane = jnp.zeros((1, LANES), F32)
    invf_lane = invf_lane.at[0, QK_NOPE:QK_NOPE + half].set(inv_freq)
    invf_lane = invf_lane.at[0, QK_NOPE + half:QK_DIM].set(inv_freq)
    sign_lane = jnp.zeros((1, LANES), F32)
    sign_lane = sign_lane.at[0, QK_NOPE:QK_NOPE + half].set(-1.0)
    sign_lane = sign_lane.at[0, QK_NOPE + half:QK_DIM].set(1.0)
    tables = _rope_tables(positions.reshape(t, 1), positions.reshape(1, t), invf_lane, sign_lane,
                          inv_freq[:, None])

    xs = x.reshape(t, D_MODEL)
    for l in range(depth):
        proj = _in_proj(xs, norm1_gain[l][None, :], w_in_p, l)
        y_a = _gm_mixer(proj, gm_v_gain[l][None, :], gm_w_s[l], bias_full[l],
                        gm_out_gain[l][None, :])
        y_b = _hg_mixer(proj, hg_lower_bound, hg_out_gain[l][None, :], l, batch, seq)
        qt, k, vt = _mla_prep(proj, tables, mla_q_a_gain[l][None, :], mla_kv_a_gain[l][None, :],
                              wqt[l], wk[l], wvt[l], q_gain_b[l], k_gain_p[l][None, :],
                              k_gain_sw[l][None, :], batch, seq)
        y_c = _mla_attn(qt, k, vt, mla_out_gain[l][None, :], batch, seq)
        xs = _out_ffn(xs, y_a, y_b, y_c, w_out_b, norm2_gain[l][None, :], w1_b, w2_b, l)
    return xs.reshape(batch, seq, D_MODEL)
```

```python
import functools

import jax
import jax.numpy as jnp
from jax import lax
from jax.experimental import pallas as pl
from jax.experimental.pallas import tpu as pltpu

F32 = jnp.float32
BF16 = jnp.bfloat16

D_MODEL = 1024
DEPTH = 4
CHUNK = 128
EPS = 1e-6
GM_HEADS = 4
HG_HEADS = 4
HEAD64 = 64
PAIR = 2 * HEAD64
MLA_HEADS = 8
QK_NOPE = 64
QK_ROPE = 32
QK_DIM = QK_NOPE + QK_ROPE
Q_LORA = 256
KV_LORA = 128
MLA_WIDTH = MLA_HEADS * HEAD64
ROPE_THETA = 10000.0
LOG2_E = 1.4426950408889634
D_FF = 4 * D_MODEL
LANES = 128
D_IN_PAD = 2048
MLA_PAD = MLA_HEADS * LANES

VMEM_LIMIT = 56 * 1024 * 1024

TM_IN = 512
TM_GM = 512
HG_CHUNKS_PER_STEP = 4
TM_MLA = 512
TQ = 256
TM_FFN = 1024
TF_FFN = 1024


def _params(*sem):
    return pltpu.CompilerParams(dimension_semantics=sem, vmem_limit_bytes=VMEM_LIMIT)


def _dot(a, b):
    return jnp.dot(a, b, preferred_element_type=F32)


def _dot_nt(a, b):
    return lax.dot_general(a, b, (((1,), (1,)), ((), ())), preferred_element_type=F32)


def _dot_tn(a, b):
    return lax.dot_general(a, b, (((0,), (0,)), ((), ())), preferred_element_type=F32)


def _split_dot(x, w_bf16, parts, w_on_left=False):
    acc = None
    rem = x
    for _ in range(parts):
        piece = rem.astype(BF16)
        term = _dot(w_bf16, piece) if w_on_left else _dot(piece, w_bf16)
        acc = term if acc is None else acc + term
        rem = rem - piece.astype(F32)
    return acc


def _head_mean_matrix(width):
    r = lax.broadcasted_iota(jnp.int32, (width, width), 0) // HEAD64
    c = lax.broadcasted_iota(jnp.int32, (width, width), 1) // HEAD64
    return jnp.where(r == c, 1.0 / HEAD64, 0.0).astype(BF16)


def _head_rms(x, gmat):
    ms = _split_dot(x * x, gmat, 2)
    return x * lax.rsqrt(ms + EPS)


def _in_proj_kernel(x_ref, g_ref, w_ref, o_ref):
    x = x_ref[...]
    ms = jnp.mean(x * x, axis=-1, keepdims=True)
    h = x * lax.rsqrt(ms + EPS) * g_ref[...]
    o_ref[...] = _dot(h.astype(BF16), w_ref[...])


def _in_proj(x, gain, w_stack, layer):
    t = x.shape[0]
    return pl.pallas_call(
        _in_proj_kernel,
        grid=(t // TM_IN,),
        in_specs=[
            pl.BlockSpec((TM_IN, D_MODEL), lambda i: (i, 0)),
            pl.BlockSpec((1, D_MODEL), lambda i: (0, 0)),
            pl.BlockSpec((None, D_MODEL, D_IN_PAD), lambda i: (layer, 0, 0)),
        ],
        out_specs=pl.BlockSpec((TM_IN, D_IN_PAD), lambda i: (i, 0)),
        out_shape=jax.ShapeDtypeStruct((t, D_IN_PAD), F32),
        compiler_params=_params("parallel"),
        name="in_proj",
    )(x, gain, w_stack)


def _gm_kernel(u_ref, v_ref, vg_ref, w_ref, b_ref, og_ref, o_ref):
    width = GM_HEADS * HEAD64
    gmat = _head_mean_matrix(width)
    u = jax.nn.gelu(u_ref[...])
    v = _head_rms(jax.nn.gelu(v_ref[...]), gmat) * vg_ref[...]
    row = lax.broadcasted_iota(jnp.int32, (CHUNK, CHUNK), 0)
    col = lax.broadcasted_iota(jnp.int32, (CHUNK, CHUNK), 1)
    causal = row >= col
    lane_head = lax.broadcasted_iota(jnp.int32, (1, PAIR), 1) // HEAD64
    w_tril = [jnp.where(causal, w_ref[h], 0.0).astype(BF16) for h in range(GM_HEADS)]
    w_pair = [jnp.concatenate([w_tril[2 * p], w_tril[2 * p + 1]], axis=1)
              for p in range(GM_HEADS // 2)]
    bias = b_ref[...]
    for c in range(TM_GM // CHUNK):
        rows = slice(c * CHUNK, (c + 1) * CHUNK)
        y_blocks = []
        for p in range(GM_HEADS // 2):
            vb = v[rows, p * PAIR:(p + 1) * PAIR]
            v_cat = jnp.concatenate(
                [jnp.where(lane_head == hh, vb, 0.0).astype(BF16) for hh in range(2)], axis=0)
            y_blocks.append(_dot(w_pair[p], v_cat))
        y = jnp.concatenate(y_blocks, axis=-1) + bias
        out = u[rows] * y
        out = _head_rms(out, gmat) * og_ref[...]
        o_ref[rows, :] = out.astype(o_ref.dtype)


def _gm_mixer(proj, v_gain, w_s, bias_full, out_gain):
    t = proj.shape[0]
    width = GM_HEADS * HEAD64
    return pl.pallas_call(
        _gm_kernel,
        grid=(t // TM_GM,),
        in_specs=[
            pl.BlockSpec((TM_GM, width), lambda i: (i, 0)),
            pl.BlockSpec((TM_GM, width), lambda i: (i, 1)),
            pl.BlockSpec((1, width), lambda i: (0, 0)),
            pl.BlockSpec((GM_HEADS, CHUNK, CHUNK), lambda i: (0, 0, 0)),
            pl.BlockSpec((CHUNK, width), lambda i: (0, 0)),
            pl.BlockSpec((1, width), lambda i: (0, 0)),
        ],
        out_specs=pl.BlockSpec((TM_GM, width), lambda i: (i, 0)),
        out_shape=jax.ShapeDtypeStruct((t, width), BF16),
        compiler_params=_params("parallel"),
        name="gm_mixer",
    )(proj, proj, v_gain, w_s, bias_full, out_gain)


def _anchor_rows(b, m):
    n = b.shape[0]
    if m >= 8:
        pieces = []
        for blk in range(n // (2 * m)):
            a = blk * 2 * m + m - 1
            pieces.append(jnp.broadcast_to(b[a:a + 1, :], (2 * m, b.shape[1])))
        return pieces[0] if len(pieces) == 1 else jnp.concatenate(pieces, axis=0)
    b3 = b.reshape(n // 8, 8, b.shape[1])
    sub = lax.broadcasted_iota(jnp.int32, (1, 8, 1), 1)
    r = None
    for blk in range(8 // (2 * m)):
        a = blk * 2 * m + m - 1
        cand = jnp.broadcast_to(b3[:, a:a + 1, :], b3.shape)
        r = cand if r is None else jnp.where(sub >= blk * 2 * m, cand, r)
    return r.reshape(n, b.shape[1])


def _hg_kernel(layer, q_ref, f_ref, i_ref, g_ref, lb_ref, og_ref, o_ref, st_ref):
    width = HG_HEADS * HEAD64
    n_pairs = HG_HEADS // 2

    @pl.when(pl.program_id(1) == 0)
    def _():
        st_ref[...] = jnp.zeros_like(st_ref)

    lbs = lb_ref[...]
    e = jnp.exp(lbs - jnp.max(lbs, axis=0, keepdims=True))
    soft = e / jnp.sum(e, axis=0, keepdims=True)
    lb = jnp.zeros((1, width), F32)
    for i in range(1, layer + 1):
        lb = lb + soft[i:i + 1, :]

    qr = q_ref[...]
    q_all = qr * jax.nn.sigmoid(qr)
    f_all = lb + (1.0 - lb) * jax.nn.sigmoid(f_ref[...])
    kk_all = 1.0 - f_all
    lf_all = jnp.log(f_all)
    v_all = i_ref[...]

    row = lax.broadcasted_iota(jnp.int32, (CHUNK, CHUNK), 0)
    col = lax.broadcasted_iota(jnp.int32, (CHUNK, CHUNK), 1)
    tri = jnp.where(row >= col, 1.0, 0.0).astype(BF16)
    lane_head = lax.broadcasted_iota(jnp.int32, (1, PAIR), 1) // HEAD64
    t_idx = lax.broadcasted_iota(jnp.int32, (CHUNK, 1), 0)
    rr = lax.broadcasted_iota(jnp.int32, (PAIR, PAIR), 0) // HEAD64
    cc = lax.broadcasted_iota(jnp.int32, (PAIR, PAIR), 1) // HEAD64
    same_head = rr == cc
    code = jnp.where(row >= col, row ^ col, -1)
    level_of = jnp.where(row >= col, 0, -1)
    upper_rows = []
    m = 1
    while m < CHUNK:
        level_of = level_of + jnp.where(code >= m, 1, 0)
        upper_rows.append((t_idx % (2 * m)) >= m)
        m *= 2
    n_levels = len(upper_rows)

    def pair(x, p):
        return x[:, p * PAIR:(p + 1) * PAIR]

    def head_only(xb, hh):
        return jnp.where(lane_head == hh, xb, 0.0).astype(BF16)

    def one_chunk(q_c, kk_c, f_c, lf_c, v_c):
        b_c = _split_dot(lf_c, tri, 3, w_on_left=True)
        o_blocks = []
        for p in range(n_pairs):
            q, kk, f, b, v = (pair(x, p) for x in (q_c, kk_c, f_c, b_c, v_c))
            def both_heads(lhs, rhs):
                stacked = jnp.concatenate([head_only(lhs, 0), head_only(lhs, 1)], axis=0)
                s2 = _dot_nt(stacked, rhs.astype(BF16))
                return s2[:CHUNK], s2[CHUNK:]

            scores = [jnp.where(level_of == 0, s, 0.0) for s in both_heads(q, kk)]
            for lvl in range(n_levels, 0, -1):
                m = 1 << (lvl - 1)
                upper = upper_rows[lvl - 1]
                if m == 1:
                    w = jnp.where(upper, f, 1.0)
                else:
                    d = b - _anchor_rows(b, m)
                    w = jnp.exp(jnp.minimum(d, -d))
                u = jnp.where(upper, q, kk) * w
                for hh, s in enumerate(both_heads(u, u)):
                    scores[hh] = jnp.where(level_of == lvl, s, scores[hh])

            b_last = b[CHUNK - 1:CHUNK, :]
            st = st_ref[p]
            o = _dot_nt((q * jnp.exp(b)).astype(BF16), st.astype(BF16))
            s_cat = jnp.concatenate([scores[0].astype(BF16), scores[1].astype(BF16)], axis=1)
            v_cat = jnp.concatenate([head_only(v, 0), head_only(v, 1)], axis=0)
            o_blocks.append(o + _dot(s_cat, v_cat))
            kd = kk * jnp.exp(jnp.minimum(b_last - b, 0.0))
            upd = _dot_tn(v.astype(BF16), kd.astype(BF16))
            st_ref[p] = st * jnp.exp(b_last) + jnp.where(same_head, upd, 0.0)
        return jnp.concatenate(o_blocks, axis=-1)

    outs = []
    for c in range(HG_CHUNKS_PER_STEP):
        rows = slice(c * CHUNK, (c + 1) * CHUNK)
        outs.append(one_chunk(q_all[rows], kk_all[rows], f_all[rows], lf_all[rows], v_all[rows]))
    o = jnp.concatenate(outs, axis=0)

    gmat = _head_mean_matrix(width)
    gr = g_ref[...]
    out = _head_rms(o, gmat) * og_ref[...] * (gr * jax.nn.sigmoid(gr))
    o_ref[...] = out.astype(o_ref.dtype)


def _hg_mixer(proj, lower_bound_raw, out_gain, layer, batch, seq):
    t = proj.shape[0]
    width = HG_HEADS * HEAD64
    rows = HG_CHUNKS_PER_STEP * CHUNK
    nc = seq // rows

    def col_block(j):
        return pl.BlockSpec((rows, width), lambda bi, ci: (bi * nc + ci, j))

    return pl.pallas_call(
        functools.partial(_hg_kernel, layer),
        grid=(batch, nc),
        in_specs=[
            col_block(2), col_block(3), col_block(4), col_block(5),
            pl.BlockSpec((DEPTH, width), lambda bi, ci: (0, 0)),
            pl.BlockSpec((1, width), lambda bi, ci: (0, 0)),
        ],
        out_specs=pl.BlockSpec((rows, width), lambda bi, ci: (bi * nc + ci, 0)),
        out_shape=jax.ShapeDtypeStruct((t, width), BF16),
        scratch_shapes=[pltpu.VMEM((HG_HEADS // 2, PAIR, PAIR), F32)],
        compiler_params=_params("parallel", "arbitrary"),
        name="hg_mixer",
    )(proj, proj, proj, proj, lower_bound_raw, out_gain)


def _rope_table_kernel(pos_ref, invf_ref, cos_t_ref, sin_t_ref, nsin_t_ref):
    ang_t = invf_ref[...] * pos_ref[...].astype(F32)
    sin_t = jnp.sin(ang_t)
    cos_t_ref[...] = jnp.cos(ang_t)
    sin_t_ref[...] = sin_t
    nsin_t_ref[...] = -sin_t


def _rope_tables(pos_row, invf_col):
    t = pos_row.shape[1]
    half = QK_ROPE // 2
    tm = 2048
    table = pl.BlockSpec((half, tm), lambda i: (0, i))
    return pl.pallas_call(
        _rope_table_kernel,
        grid=(t // tm,),
        in_specs=[pl.BlockSpec((1, tm), lambda i: (0, i)), pl.BlockSpec((half, 1), lambda i: (0, 0))],
        out_specs=[table] * 3,
        out_shape=[jax.ShapeDtypeStruct((half, t), F32)] * 3,
        compiler_params=_params("parallel"),
        name="rope_tables",
    )(pos_row, invf_col)


def _swap_rope_halves(x):
    lane = lax.broadcasted_iota(jnp.int32, (1, LANES), 1)
    half = QK_ROPE // 2
    from_above = pltpu.roll(x, LANES - half, axis=1)
    from_below = pltpu.roll(x, half, axis=1)
    return jnp.where(lane < QK_NOPE + half, from_above, from_below)


def _mla_prep_kernel(cq_ref, ckv_ref, kpe_ref, cos_ref, sin_ref, cos_t_ref, sin_t_ref,
                     qag_ref, kvag_ref, wqt_ref, wk_ref, wvt_ref, qg_ref, kg_ref, kgs_ref,
                     qt_out, k_out, vt_out):
    def rms(x, gain):
        ms = jnp.mean(x * x, axis=-1, keepdims=True)
        return x * lax.rsqrt(ms + EPS) * gain

    half = QK_ROPE // 2
    hq = rms(cq_ref[...], qag_ref[...]).astype(BF16)
    hkv = rms(ckv_ref[...], kvag_ref[...]).astype(BF16)

    qt = _dot_nt(wqt_ref[...], hq)
    cos_t = cos_t_ref[...]
    sin_t = sin_t_ref[...]
    for h in range(MLA_HEADS):
        x = qt[h * QK_DIM:(h + 1) * QK_DIM, :]
        ms = jnp.sum(x * x, axis=0, keepdims=True) * (1.0 / QK_DIM)
        xn = x * lax.rsqrt(ms + EPS) * qg_ref[...]
        r1 = xn[QK_NOPE:QK_NOPE + half]
        r2 = xn[QK_NOPE + half:]
        base = h * LANES
        qt_out[base:base + QK_NOPE, :] = xn[:QK_NOPE].astype(qt_out.dtype)
        qt_out[base + QK_NOPE:base + QK_NOPE + half, :] = (r1 * cos_t - r2 * sin_t).astype(qt_out.dtype)
        qt_out[base + QK_NOPE + half:base + QK_DIM, :] = (r2 * cos_t + r1 * sin_t).astype(qt_out.dtype)
        qt_out[base + QK_DIM:base + LANES, :] = jnp.zeros((LANES - QK_DIM, x.shape[1]), qt_out.dtype)

    vt = _dot_nt(wvt_ref[...], hkv).astype(vt_out.dtype)
    ones = jnp.ones((HEAD64, vt.shape[1]), vt_out.dtype)
    for h in range(MLA_HEADS):
        vals = vt[h * HEAD64:(h + 1) * HEAD64, :]
        lo, hi = (vals, ones) if h % 2 == 0 else (ones, vals)
        vt_out[h * LANES:h * LANES + HEAD64, :] = lo
        vt_out[h * LANES + HEAD64:(h + 1) * LANES, :] = hi

    kn = _dot(hkv, wk_ref[...])
    kpe = pltpu.roll(kpe_ref[...], QK_NOPE, axis=1)
    gain_cos = kg_ref[...] * cos_ref[...]
    rot = _swap_rope_halves(kpe) * kgs_ref[...] * sin_ref[...]
    for h in range(MLA_HEADS):
        lanes = slice(h * LANES, (h + 1) * LANES)
        x = kn[:, lanes] + kpe
        ms = jnp.sum(x * x, axis=-1, keepdims=True) * (1.0 / QK_DIM)
        k_out[:, lanes] = (lax.rsqrt(ms + EPS) * (x * gain_cos + rot)).astype(k_out.dtype)


def _mla_prep(proj, tables, qa_gain, kva_gain, wqt, wk, wvt, q_gain_b, k_gain, k_gain_sw,
              batch, seq):
    t = proj.shape[0]
    half = QK_ROPE // 2
    tiles_per_seq = seq // TM_MLA
    cos_n, sin_n, cos_t, sin_t = tables
    full = lambda shape: pl.BlockSpec(shape, lambda i: (0,) * len(shape))
    out = jax.ShapeDtypeStruct((t, MLA_PAD), BF16)
    out_t = jax.ShapeDtypeStruct((batch, MLA_PAD, seq), BF16)
    spec_t = pl.BlockSpec((None, MLA_PAD, TM_MLA),
                          lambda i: (i // tiles_per_seq, 0, i % tiles_per_seq))
    return pl.pallas_call(
        _mla_prep_kernel,
        grid=(t // TM_MLA,),
        in_specs=[
            pl.BlockSpec((TM_MLA, Q_LORA), lambda i: (i, 6)),
            pl.BlockSpec((TM_MLA, KV_LORA), lambda i: (i, 14)),
            pl.BlockSpec((TM_MLA, LANES), lambda i: (i, 15)),
            pl.BlockSpec((TM_MLA, LANES), lambda i: (i, 0)),
            pl.BlockSpec((TM_MLA, LANES), lambda i: (i, 0)),
            pl.BlockSpec((half, TM_MLA), lambda i: (0, i)),
            pl.BlockSpec((half, TM_MLA), lambda i: (0, i)),
            full((1, Q_LORA)), full((1, KV_LORA)),
            full((MLA_HEADS * QK_DIM, Q_LORA)), full((KV_LORA, MLA_PAD)),
            full((MLA_WIDTH, KV_LORA)),
            full((QK_DIM, TM_MLA)), full((1, LANES)), full((1, LANES)),
        ],
        out_specs=[spec_t, pl.BlockSpec((TM_MLA, MLA_PAD), lambda i: (i, 0)), spec_t],
        out_shape=[out_t, out, out_t],
        compiler_params=_params("parallel"),
        name="mla_prep",
    )(proj, proj, proj, cos_n, sin_n, cos_t, sin_t, qa_gain, kva_gain, wqt, wk, wvt,
      q_gain_b, k_gain, k_gain_sw)


def _mla_attn_kernel(qt_ref, k_ref, vt_ref, og_ref, o_ref, st_ref, acc_ref):
    seq = k_ref.shape[0]
    row = lax.broadcasted_iota(jnp.int32, (TQ, TQ), 0)
    col = lax.broadcasted_iota(jnp.int32, (TQ, TQ), 1)
    causal_t = col >= row
    low = lax.broadcasted_iota(jnp.int32, (LANES, 1), 0) < HEAD64
    gain = og_ref[...]

    for qi in range(seq // TQ):
        cols = slice(qi * TQ, (qi + 1) * TQ)
        n_keys = (qi + 1) * TQ
        spans = [(s, min(2 * TQ, n_keys - s)) for s in range(0, n_keys, 2 * TQ)]
        maxes = [None] * MLA_HEADS
        for start, rows in spans:
            for h in range(MLA_HEADS):
                lanes = slice(h * LANES, (h + 1) * LANES)
                st = _dot(k_ref[start:start + rows, lanes], qt_ref[lanes, cols])
                if start + rows == n_keys:
                    last = jnp.where(causal_t, st[rows - TQ:], -jnp.inf)
                    st = last if rows == TQ else jnp.concatenate([st[:rows - TQ], last], axis=0)
                st_ref[h, start:start + rows, :] = st
                cm = jnp.max(st, axis=0, keepdims=True)
                maxes[h] = cm if maxes[h] is None else jnp.maximum(maxes[h], cm)
        for i, (start, rows) in enumerate(spans):
            for h in range(MLA_HEADS):
                lanes = slice(h * LANES, (h + 1) * LANES)
                x = st_ref[h, start:start + rows, :] - maxes[h]
                pt = jnp.exp2(x.astype(BF16))
                part = _dot(vt_ref[lanes, start:start + rows], pt)
                if i == 0:
                    acc_ref[h] = part
                else:
                    acc_ref[h] += part
        pairs = []
        for p in range(MLA_HEADS // 2):
            acc_e = acc_ref[2 * p]
            acc_o = acc_ref[2 * p + 1]
            ot = jnp.where(low, acc_e / acc_e[HEAD64:HEAD64 + 1, :], acc_o / acc_o[0:1, :])
            sq = ot * ot
            ms_e = jnp.sum(sq[:HEAD64], axis=0, keepdims=True) * (1.0 / HEAD64)
            ms_o = jnp.sum(sq[HEAD64:], axis=0, keepdims=True) * (1.0 / HEAD64)
            ot = ot * lax.rsqrt(jnp.where(low, ms_e, ms_o) + EPS)
            pairs.append(ot.T)
        o_ref[cols, :] = (jnp.concatenate(pairs, axis=-1) * gain).astype(o_ref.dtype)


def _mla_attn(qt, k, vt, out_gain, batch, seq):
    t = k.shape[0]
    return pl.pallas_call(
        _mla_attn_kernel,
        grid=(batch,),
        in_specs=[
            pl.BlockSpec((None, MLA_PAD, seq), lambda bi: (bi, 0, 0)),
            pl.BlockSpec((seq, MLA_PAD), lambda bi: (bi, 0)),
            pl.BlockSpec((None, MLA_PAD, seq), lambda bi: (bi, 0, 0)),
            pl.BlockSpec((1, MLA_WIDTH), lambda bi: (0, 0)),
        ],
        out_specs=pl.BlockSpec((seq, MLA_WIDTH), lambda bi: (bi, 0)),
        out_shape=jax.ShapeDtypeStruct((t, MLA_WIDTH), BF16),
        scratch_shapes=[pltpu.VMEM((MLA_HEADS, seq, TQ), F32),
                        pltpu.VMEM((MLA_HEADS, LANES, TQ), F32)],
        compiler_params=_params("parallel"),
        name="mla_attn",
    )(qt, k, vt, out_gain)


def _out_ffn_kernel(x_ref, ya_ref, yb_ref, yc_ref, woa_ref, wob_ref, woc_ref, g_ref,
                    w1_ref, w2_ref, o_ref, h_ref):
    j = pl.program_id(1)

    @pl.when(j == 0)
    def _():
        xn = (x_ref[...] + _dot(ya_ref[...], woa_ref[...]) + _dot(yb_ref[...], wob_ref[...])
              + _dot(yc_ref[...], woc_ref[...]))
        ms = jnp.mean(xn * xn, axis=-1, keepdims=True)
        h_ref[...] = (xn * lax.rsqrt(ms + EPS) * g_ref[...]).astype(h_ref.dtype)
        o_ref[...] = xn

    a = jnp.maximum(_dot(h_ref[...], w1_ref[...]), 0.0)
    o_ref[...] += _dot((a * a).astype(BF16), w2_ref[...])


def _out_ffn(x, ya, yb, yc, w_out_stack, gain2, w1_stack, w2_stack, layer):
    t = x.shape[0]
    wa, wb, wc = ya.shape[1], yb.shape[1], yc.shape[1]
    assert wa == wb and wc == 2 * wa
    row = lambda width: pl.BlockSpec((TM_FFN, width), lambda i, j: (i, 0))
    wo_rows = lambda width, blk: pl.BlockSpec((None, width, D_MODEL), lambda i, j: (layer, blk, 0))
    return pl.pallas_call(
        _out_ffn_kernel,
        grid=(t // TM_FFN, D_FF // TF_FFN),
        in_specs=[
            row(D_MODEL), row(wa), row(wb), row(wc),
            wo_rows(wa, 0), wo_rows(wb, 1), wo_rows(wc, 1),
            pl.BlockSpec((1, D_MODEL), lambda i, j: (0, 0)),
            pl.BlockSpec((None, D_MODEL, TF_FFN), lambda i, j: (layer, 0, j)),
            pl.BlockSpec((None, TF_FFN, D_MODEL), lambda i, j: (layer, j, 0)),
        ],
        out_specs=row(D_MODEL),
        out_shape=jax.ShapeDtypeStruct((t, D_MODEL), F32),
        scratch_shapes=[pltpu.VMEM((TM_FFN, D_MODEL), BF16)],
        compiler_params=_params("parallel", "arbitrary"),
        name="out_ffn",
    )(x, ya, yb, yc, w_out_stack, w_out_stack, w_out_stack, gain2, w1_stack, w2_stack)


def kernel(x, positions, norm1_gain, w_in, gm_v_gain, gm_w_s, gm_b_s, gm_out_gain,
           hg_lower_bound, hg_out_gain, mla_q_a_gain, mla_w_uq, mla_kv_a_gain, mla_w_ukv,
           mla_q_gain, mla_k_gain, mla_out_gain, w_out, norm2_gain, w_ff1, w_ff2):
    batch, seq, _ = x.shape
    t = batch * seq
    depth = w_in.shape[0]
    gm_w = GM_HEADS * HEAD64

    w_in_p = jnp.pad(w_in.astype(BF16), ((0, 0), (0, 0), (0, D_IN_PAD - w_in.shape[2])))

    half = QK_ROPE // 2
    wqt = jnp.swapaxes(mla_w_uq, 1, 2).astype(BF16)
    wkv = mla_w_ukv.astype(BF16).reshape(depth, KV_LORA, MLA_HEADS, QK_NOPE + HEAD64)
    wk = jnp.pad(wkv[..., :QK_NOPE], ((0, 0),) * 3 + ((0, LANES - QK_NOPE),))
    wk = wk.reshape(depth, KV_LORA, MLA_PAD)
    wvt = jnp.swapaxes(wkv[..., QK_NOPE:].reshape(depth, KV_LORA, MLA_WIDTH), 1, 2)
    q_gain_b = jnp.broadcast_to((mla_q_gain * (QK_DIM ** -0.5 * LOG2_E))[:, :, None],
                                (depth, QK_DIM, TM_MLA))
    pad_lanes = ((0, 0), (0, LANES - QK_DIM))
    k_gain_p = jnp.pad(mla_k_gain, pad_lanes)
    k_gain_sw = jnp.pad(jnp.concatenate(
        [mla_k_gain[:, :QK_NOPE], mla_k_gain[:, QK_NOPE + half:], mla_k_gain[:, QK_NOPE:QK_NOPE + half]],
        axis=-1), pad_lanes)

    w_out_b = w_out.astype(BF16)
    w1_b = w_ff1.astype(BF16)
    w2_b = w_ff2.astype(BF16)
    bias_full = jnp.broadcast_to(jnp.swapaxes(gm_b_s, 1, 2)[..., None],
                                 (depth, CHUNK, GM_HEADS, HEAD64)).reshape(depth, CHUNK, gm_w)

    inv_freq = ROPE_THETA ** (-jnp.arange(half, dtype=F32) / half)
    cos_t, sin_t, nsin_t = _rope_tables(positions.reshape(1, t), inv_freq[:, None])
    cos_n = jnp.concatenate([jnp.ones((t, QK_NOPE), F32), cos_t.T, cos_t.T,
                             jnp.ones((t, LANES - QK_DIM), F32)], axis=1)
    sin_n = jnp.concatenate([jnp.zeros((t, QK_NOPE), F32), nsin_t.T, sin_t.T,
                             jnp.zeros((t, LANES - QK_DIM), F32)], axis=1)
    tables = (cos_n, sin_n, cos_t, sin_t)

    xs = x.reshape(t, D_MODEL)
    for l in range(depth):
        proj = _in_proj(xs, norm1_gain[l][None, :], w_in_p, l)
        y_a = _gm_mixer(proj, gm_v_gain[l][None, :], gm_w_s[l], bias_full[l],
                        gm_out_gain[l][None, :])
        y_b = _hg_mixer(proj, hg_lower_bound, hg_out_gain[l][None, :], l, batch, seq)
        qt, k, vt = _mla_prep(proj, tables, mla_q_a_gain[l][None, :], mla_kv_a_gain[l][None, :],
                              wqt[l], wk[l], wvt[l], q_gain_b[l], k_gain_p[l][None, :],
                              k_gain_sw[l][None, :], batch, seq)
        y_c = _mla_attn(qt, k, vt, mla_out_gain[l][None, :], batch, seq)
        xs = _out_ffn(xs, y_a, y_b, y_c, w_out_b, norm2_gain[l][None, :], w1_b, w2_b, l)
    return xs.reshape(batch, seq, D_MODEL)
```

```python
import functools

import jax
import jax.numpy as jnp
from jax import lax
from jax.experimental import pallas as pl
from jax.experimental.pallas import tpu as pltpu

F32 = jnp.float32
BF16 = jnp.bfloat16

D_MODEL = 1024
DEPTH = 4
CHUNK = 128
EPS = 1e-6
GM_HEADS = 4
HG_HEADS = 4
HEAD64 = 64
PAIR = 2 * HEAD64
MLA_HEADS = 8
QK_NOPE = 64
QK_ROPE = 32
QK_DIM = QK_NOPE + QK_ROPE
Q_LORA = 256
KV_LORA = 128
MLA_WIDTH = MLA_HEADS * HEAD64
ROPE_THETA = 10000.0
LOG2_E = 1.4426950408889634
D_FF = 4 * D_MODEL
LANES = 128
D_IN_PAD = 2048
MLA_PAD = MLA_HEADS * LANES

VMEM_LIMIT = 56 * 1024 * 1024

TM_IN = 512
TM_GM = 512
HG_CHUNKS_PER_STEP = 4
TM_MLA = 512
TQ = 256
TM_FFN = 1024
TF_FFN = 1024


def _params(*sem):
    return pltpu.CompilerParams(dimension_semantics=sem, vmem_limit_bytes=VMEM_LIMIT)


def _dot(a, b):
    return jnp.dot(a, b, preferred_element_type=F32)


def _dot_nt(a, b):
    return lax.dot_general(a, b, (((1,), (1,)), ((), ())), preferred_element_type=F32)


def _dot_tn(a, b):
    return lax.dot_general(a, b, (((0,), (0,)), ((), ())), preferred_element_type=F32)


def _split_dot(x, w_bf16, parts, w_on_left=False):
    acc = None
    rem = x
    for _ in range(parts):
        piece = rem.astype(BF16)
        term = _dot(w_bf16, piece) if w_on_left else _dot(piece, w_bf16)
        acc = term if acc is None else acc + term
        rem = rem - piece.astype(F32)
    return acc


def _head_mean_matrix(width):
    r = lax.broadcasted_iota(jnp.int32, (width, width), 0) // HEAD64
    c = lax.broadcasted_iota(jnp.int32, (width, width), 1) // HEAD64
    return jnp.where(r == c, 1.0 / HEAD64, 0.0).astype(BF16)


def _head_rms(x, gmat):
    ms = _split_dot(x * x, gmat, 2)
    return x * lax.rsqrt(ms + EPS)


def _in_proj_kernel(x_ref, g_ref, w_ref, o_ref):
    x = x_ref[...]
    ms = jnp.mean(x * x, axis=-1, keepdims=True)
    h = x * lax.rsqrt(ms + EPS) * g_ref[...]
    o_ref[...] = _dot(h.astype(BF16), w_ref[...])


def _in_proj(x, gain, w_stack, layer):
    t = x.shape[0]
    return pl.pallas_call(
        _in_proj_kernel,
        grid=(t // TM_IN,),
        in_specs=[
            pl.BlockSpec((TM_IN, D_MODEL), lambda i: (i, 0)),
            pl.BlockSpec((1, D_MODEL), lambda i: (0, 0)),
            pl.BlockSpec((None, D_MODEL, D_IN_PAD), lambda i: (layer, 0, 0)),
        ],
        out_specs=pl.BlockSpec((TM_IN, D_IN_PAD), lambda i: (i, 0)),
        out_shape=jax.ShapeDtypeStruct((t, D_IN_PAD), F32),
        compiler_params=_params("parallel"),
        name="in_proj",
    )(x, gain, w_stack)


def _gm_kernel(u_ref, v_ref, vg_ref, w_ref, b_ref, og_ref, o_ref):
    width = GM_HEADS * HEAD64
    gmat = _head_mean_matrix(width)
    u = jax.nn.gelu(u_ref[...])
    v = _head_rms(jax.nn.gelu(v_ref[...]), gmat) * vg_ref[...]
    row = lax.broadcasted_iota(jnp.int32, (CHUNK, CHUNK), 0)
    col = lax.broadcasted_iota(jnp.int32, (CHUNK, CHUNK), 1)
    causal = row >= col
    lane_head = lax.broadcasted_iota(jnp.int32, (1, PAIR), 1) // HEAD64
    w_tril = [jnp.where(causal, w_ref[h], 0.0).astype(BF16) for h in range(GM_HEADS)]
    w_pair = [jnp.concatenate([w_tril[2 * p], w_tril[2 * p + 1]], axis=1)
              for p in range(GM_HEADS // 2)]
    bias = b_ref[...]
    for c in range(TM_GM // CHUNK):
        rows = slice(c * CHUNK, (c + 1) * CHUNK)
        y_blocks = []
        for p in range(GM_HEADS // 2):
            vb = v[rows, p * PAIR:(p + 1) * PAIR]
            v_cat = jnp.concatenate(
                [jnp.where(lane_head == hh, vb, 0.0).astype(BF16) for hh in range(2)], axis=0)
            y_blocks.append(_dot(w_pair[p], v_cat))
        y = jnp.concatenate(y_blocks, axis=-1) + bias
        out = u[rows] * y
        out = _head_rms(out, gmat) * og_ref[...]
        o_ref[rows, :] = out.astype(o_ref.dtype)


def _gm_mixer(proj, v_gain, w_s, bias_full, out_gain):
    t = proj.shape[0]
    width = GM_HEADS * HEAD64
    return pl.pallas_call(
        _gm_kernel,
        grid=(t // TM_GM,),
        in_specs=[
            pl.BlockSpec((TM_GM, width), lambda i: (i, 0)),
            pl.BlockSpec((TM_GM, width), lambda i: (i, 1)),
            pl.BlockSpec((1, width), lambda i: (0, 0)),
            pl.BlockSpec((GM_HEADS, CHUNK, CHUNK), lambda i: (0, 0, 0)),
            pl.BlockSpec((CHUNK, width), lambda i: (0, 0)),
            pl.BlockSpec((1, width), lambda i: (0, 0)),
        ],
        out_specs=pl.BlockSpec((TM_GM, width), lambda i: (i, 0)),
        out_shape=jax.ShapeDtypeStruct((t, width), BF16),
        compiler_params=_params("parallel"),
        name="gm_mixer",
    )(proj, proj, v_gain, w_s, bias_full, out_gain)


def _anchor_rows(b, m):
    n = b.shape[0]
    if m >= 8:
        pieces = []
        for blk in range(n // (2 * m)):
            a = blk * 2 * m + m - 1
            pieces.append(jnp.broadcast_to(b[a:a + 1, :], (2 * m, b.shape[1])))
        return pieces[0] if len(pieces) == 1 else jnp.concatenate(pieces, axis=0)
    b3 = b.reshape(n // 8, 8, b.shape[1])
    sub = lax.broadcasted_iota(jnp.int32, (1, 8, 1), 1)
    r = None
    for blk in range(8 // (2 * m)):
        a = blk * 2 * m + m - 1
        cand = jnp.broadcast_to(b3[:, a:a + 1, :], b3.shape)
        r = cand if r is None else jnp.where(sub >= blk * 2 * m, cand, r)
    return r.reshape(n, b.shape[1])


def _hg_kernel(layer, q_ref, f_ref, i_ref, g_ref, lb_ref, og_ref, o_ref, st_ref):
    width = HG_HEADS * HEAD64
    n_pairs = HG_HEADS // 2

    @pl.when(pl.program_id(1) == 0)
    def _():
        st_ref[...] = jnp.zeros_like(st_ref)

    lbs = lb_ref[...]
    e = jnp.exp(lbs - jnp.max(lbs, axis=0, keepdims=True))
    soft = e / jnp.sum(e, axis=0, keepdims=True)
    lb = jnp.zeros((1, width), F32)
    for i in range(1, layer + 1):
        lb = lb + soft[i:i + 1, :]

    qr = q_ref[...]
    q_all = qr * jax.nn.sigmoid(qr)
    f_all = lb + (1.0 - lb) * jax.nn.sigmoid(f_ref[...])
    kk_all = 1.0 - f_all
    lf_all = jnp.log(f_all)
    v_all = i_ref[...]

    row = lax.broadcasted_iota(jnp.int32, (CHUNK, CHUNK), 0)
    col = lax.broadcasted_iota(jnp.int32, (CHUNK, CHUNK), 1)
    tri = jnp.where(row >= col, 1.0, 0.0).astype(BF16)
    lane_head = lax.broadcasted_iota(jnp.int32, (1, PAIR), 1) // HEAD64
    t_idx = lax.broadcasted_iota(jnp.int32, (CHUNK, 1), 0)
    rr = lax.broadcasted_iota(jnp.int32, (PAIR, PAIR), 0) // HEAD64
    cc = lax.broadcasted_iota(jnp.int32, (PAIR, PAIR), 1) // HEAD64
    same_head = rr == cc
    code = jnp.where(row >= col, row ^ col, -1)
    level_of = jnp.where(row >= col, 0, -1)
    upper_rows = []
    m = 1
    while m < CHUNK:
        level_of = level_of + jnp.where(code >= m, 1, 0)
        upper_rows.append((t_idx % (2 * m)) >= m)
        m *= 2
    n_levels = len(upper_rows)

    def pair(x, p):
        return x[:, p * PAIR:(p + 1) * PAIR]

    def head_only(xb, hh):
        return jnp.where(lane_head == hh, xb, 0.0).astype(BF16)

    def one_chunk(q_c, kk_c, f_c, lf_c, v_c):
        b_c = _split_dot(lf_c, tri, 3, w_on_left=True)
        o_blocks = []
        for p in range(n_pairs):
            q, kk, f, b, v = (pair(x, p) for x in (q_c, kk_c, f_c, b_c, v_c))
            def both_heads(lhs, rhs):
                stacked = jnp.concatenate([head_only(lhs, 0), head_only(lhs, 1)], axis=0)
                s2 = _dot_nt(stacked, rhs.astype(BF16))
                return s2[:CHUNK], s2[CHUNK:]

            scores = [jnp.where(level_of == 0, s, 0.0) for s in both_heads(q, kk)]
            for lvl in range(n_levels, 0, -1):
                m = 1 << (lvl - 1)
                upper = upper_rows[lvl - 1]
                if m == 1:
                    w = jnp.where(upper, f, 1.0)
                else:
                    d = b - _anchor_rows(b, m)
                    w = jnp.exp(jnp.minimum(d, -d))
                u = jnp.where(upper, q, kk) * w
                for hh, s in enumerate(both_heads(u, u)):
                    scores[hh] = jnp.where(level_of == lvl, s, scores[hh])

            b_last = b[CHUNK - 1:CHUNK, :]
            st = st_ref[p]
            o = _dot_nt((q * jnp.exp(b)).astype(BF16), st.astype(BF16))
            s_cat = jnp.concatenate([scores[0].astype(BF16), scores[1].astype(BF16)], axis=1)
            v_cat = jnp.concatenate([head_only(v, 0), head_only(v, 1)], axis=0)
            o_blocks.append(o + _dot(s_cat, v_cat))
            kd = kk * jnp.exp(jnp.minimum(b_last - b, 0.0))
            upd = _dot_tn(v.astype(BF16), kd.astype(BF16))
            st_ref[p] = st * jnp.exp(b_last) + jnp.where(same_head, upd, 0.0)
        return jnp.concatenate(o_blocks, axis=-1)

    outs = []
    for c in range(HG_CHUNKS_PER_STEP):
        rows = slice(c * CHUNK, (c + 1) * CHUNK)
        outs.append(one_chunk(q_all[rows], kk_all[rows], f_all[rows], lf_all[rows], v_all[rows]))
    o = jnp.concatenate(outs, axis=0)

    gmat = _head_mean_matrix(width)
    gr = g_ref[...]
    out = _head_rms(o, gmat) * og_ref[...] * (gr * jax.nn.sigmoid(gr))
    o_ref[...] = out.astype(o_ref.dtype)


def _hg_mixer(proj, lower_bound_raw, out_gain, layer, batch, seq):
    t = proj.shape[0]
    width = HG_HEADS * HEAD64
    rows = HG_CHUNKS_PER_STEP * CHUNK
    nc = seq // rows

    def col_block(j):
        return pl.BlockSpec((rows, width), lambda bi, ci: (bi * nc + ci, j))

    return pl.pallas_call(
        functools.partial(_hg_kernel, layer),
        grid=(batch, nc),
        in_specs=[
            col_block(2), col_block(3), col_block(4), col_block(5),
            pl.BlockSpec((DEPTH, width), lambda bi, ci: (0, 0)),
            pl.BlockSpec((1, width), lambda bi, ci: (0, 0)),
        ],
        out_specs=pl.BlockSpec((rows, width), lambda bi, ci: (bi * nc + ci, 0)),
        out_shape=jax.ShapeDtypeStruct((t, width), BF16),
        scratch_shapes=[pltpu.VMEM((HG_HEADS // 2, PAIR, PAIR), F32)],
        compiler_params=_params("parallel", "arbitrary"),
        name="hg_mixer",
    )(proj, proj, proj, proj, lower_bound_raw, out_gain)


def _rope_table_kernel(pos_ref, invf_ref, cos_t_ref, sin_t_ref, nsin_t_ref):
    ang_t = invf_ref[...] * pos_ref[...].astype(F32)
    sin_t = jnp.sin(ang_t)
    cos_t_ref[...] = jnp.cos(ang_t)
    sin_t_ref[...] = sin_t
    nsin_t_ref[...] = -sin_t


def _rope_tables(pos_row, invf_col):
    t = pos_row.shape[1]
    half = QK_ROPE // 2
    tm = 2048
    table = pl.BlockSpec((half, tm), lambda i: (0, i))
    return pl.pallas_call(
        _rope_table_kernel,
        grid=(t // tm,),
        in_specs=[pl.BlockSpec((1, tm), lambda i: (0, i)), pl.BlockSpec((half, 1), lambda i: (0, 0))],
        out_specs=[table] * 3,
        out_shape=[jax.ShapeDtypeStruct((half, t), F32)] * 3,
        compiler_params=_params("parallel"),
        name="rope_tables",
    )(pos_row, invf_col)


def _swap_rope_halves(x):
    lane = lax.broadcasted_iota(jnp.int32, (1, LANES), 1)
    half = QK_ROPE // 2
    from_above = pltpu.roll(x, LANES - half, axis=1)
    from_below = pltpu.roll(x, half, axis=1)
    return jnp.where(lane < QK_NOPE + half, from_above, from_below)


def _mla_prep_kernel(cq_ref, ckv_ref, kpe_ref, cos_ref, sin_ref, cos_t_ref, sin_t_ref,
                     qag_ref, kvag_ref, wqt_ref, wk_ref, wvt_ref, qg_ref, kg_ref, kgs_ref,
                     qt_out, k_out, vt_out):
    def rms(x, gain):
        ms = jnp.mean(x * x, axis=-1, keepdims=True)
        return x * lax.rsqrt(ms + EPS) * gain

    half = QK_ROPE // 2
    hq = rms(cq_ref[...], qag_ref[...]).astype(BF16)
    hkv = rms(ckv_ref[...], kvag_ref[...]).astype(BF16)

    qt = _dot_nt(wqt_ref[...], hq)
    cos_t = cos_t_ref[...]
    sin_t = sin_t_ref[...]
    for h in range(MLA_HEADS):
        x = qt[h * QK_DIM:(h + 1) * QK_DIM, :]
        ms = jnp.sum(x * x, axis=0, keepdims=True) * (1.0 / QK_DIM)
        xn = x * lax.rsqrt(ms + EPS) * qg_ref[...]
        r1 = xn[QK_NOPE:QK_NOPE + half]
        r2 = xn[QK_NOPE + half:]
        base = h * LANES
        qt_out[base:base + QK_NOPE, :] = xn[:QK_NOPE].astype(qt_out.dtype)
        qt_out[base + QK_NOPE:base + QK_NOPE + half, :] = (r1 * cos_t - r2 * sin_t).astype(qt_out.dtype)
        qt_out[base + QK_NOPE + half:base + QK_DIM, :] = (r2 * cos_t + r1 * sin_t).astype(qt_out.dtype)
        qt_out[base + QK_DIM:base + LANES, :] = jnp.zeros((LANES - QK_DIM, x.shape[1]), qt_out.dtype)

    vt = _dot_nt(wvt_ref[...], hkv).astype(vt_out.dtype)
    ones = jnp.ones((HEAD64, vt.shape[1]), vt_out.dtype)
    for h in range(MLA_HEADS):
        vals = vt[h * HEAD64:(h + 1) * HEAD64, :]
        lo, hi = (vals, ones) if h % 2 == 0 else (ones, vals)
        vt_out[h * LANES:h * LANES + HEAD64, :] = lo
        vt_out[h * LANES + HEAD64:(h + 1) * LANES, :] = hi

    kn = _dot(hkv, wk_ref[...])
    kpe = pltpu.roll(kpe_ref[...], QK_NOPE, axis=1)
    gain_cos = kg_ref[...] * cos_ref[...]
    rot = _swap_rope_halves(kpe) * kgs_ref[...] * sin_ref[...]
    for h in range(MLA_HEADS):
        lanes = slice(h * LANES, (h + 1) * LANES)
        x = kn[:, lanes] + kpe
        ms = jnp.sum(x * x, axis=-1, keepdims=True) * (1.0 / QK_DIM)
        k_out[:, lanes] = (lax.rsqrt(ms + EPS) * (x * gain_cos + rot)).astype(k_out.dtype)


def _mla_prep(proj, tables, qa_gain, kva_gain, wqt, wk, wvt, q_gain_b, k_gain, k_gain_sw,
              batch, seq):
    t = proj.shape[0]
    half = QK_ROPE // 2
    tiles_per_seq = seq // TM_MLA
    cos_n, sin_n, cos_t, sin_t = tables
    full = lambda shape: pl.BlockSpec(shape, lambda i: (0,) * len(shape))
    out = jax.ShapeDtypeStruct((t, MLA_PAD), BF16)
    out_t = jax.ShapeDtypeStruct((batch, MLA_PAD, seq), BF16)
    spec_t = pl.BlockSpec((None, MLA_PAD, TM_MLA),
                          lambda i: (i // tiles_per_seq, 0, i % tiles_per_seq))
    return pl.pallas_call(
        _mla_prep_kernel,
        grid=(t // TM_MLA,),
        in_specs=[
            pl.BlockSpec((TM_MLA, Q_LORA), lambda i: (i, 6)),
            pl.BlockSpec((TM_MLA, KV_LORA), lambda i: (i, 14)),
            pl.BlockSpec((TM_MLA, LANES), lambda i: (i, 15)),
            pl.BlockSpec((TM_MLA, LANES), lambda i: (i, 0)),
            pl.BlockSpec((TM_MLA, LANES), lambda i: (i, 0)),
            pl.BlockSpec((half, TM_MLA), lambda i: (0, i)),
            pl.BlockSpec((half, TM_MLA), lambda i: (0, i)),
            full((1, Q_LORA)), full((1, KV_LORA)),
            full((MLA_HEADS * QK_DIM, Q_LORA)), full((KV_LORA, MLA_PAD)),
            full((MLA_WIDTH, KV_LORA)),
            full((QK_DIM, TM_MLA)), full((1, LANES)), full((1, LANES)),
        ],
        out_specs=[spec_t, pl.BlockSpec((TM_MLA, MLA_PAD), lambda i: (i, 0)), spec_t],
        out_shape=[out_t, out, out_t],
        compiler_params=_params("parallel"),
        name="mla_prep",
    )(proj, proj, proj, cos_n, sin_n, cos_t, sin_t, qa_gain, kva_gain, wqt, wk, wvt,
      q_gain_b, k_gain, k_gain_sw)


def _mla_attn_kernel(qt_ref, k_ref, vt_ref, og_ref, o_ref, st_ref):
    seq = k_ref.shape[0]
    row = lax.broadcasted_iota(jnp.int32, (TQ, TQ), 0)
    col = lax.broadcasted_iota(jnp.int32, (TQ, TQ), 1)
    causal_t = col >= row
    low = lax.broadcasted_iota(jnp.int32, (LANES, 1), 0) < HEAD64
    gain = og_ref[...]

    for qi in range(seq // TQ):
        cols = slice(qi * TQ, (qi + 1) * TQ)
        n_keys = (qi + 1) * TQ
        spans = [(s, min(2 * TQ, n_keys - s)) for s in range(0, n_keys, 2 * TQ)]
        maxes = [None] * MLA_HEADS
        for start, rows in spans:
            for h in range(MLA_HEADS):
                lanes = slice(h * LANES, (h + 1) * LANES)
                st = _dot(k_ref[start:start + rows, lanes], qt_ref[lanes, cols])
                if start + rows == n_keys:
                    last = jnp.where(causal_t, st[rows - TQ:], -jnp.inf)
                    st = last if rows == TQ else jnp.concatenate([st[:rows - TQ], last], axis=0)
                st_ref[h, start:start + rows, :] = st
                cm = jnp.max(st, axis=0, keepdims=True)
                maxes[h] = cm if maxes[h] is None else jnp.maximum(maxes[h], cm)
        accs = []
        for h in range(MLA_HEADS):
            lanes = slice(h * LANES, (h + 1) * LANES)
            acc = None
            for start, rows in spans:
                x = st_ref[h, start:start + rows, :] - maxes[h]
                pt = jnp.exp2(x.astype(BF16))
                part = _dot(vt_ref[lanes, start:start + rows], pt)
                acc = part if acc is None else acc + part
            accs.append(acc)
        pairs = []
        for p in range(MLA_HEADS // 2):
            acc_e = accs[2 * p]
            acc_o = accs[2 * p + 1]
            ot = jnp.where(low, acc_e / acc_e[HEAD64:HEAD64 + 1, :], acc_o / acc_o[0:1, :])
            sq = ot * ot
            ms_e = jnp.sum(sq[:HEAD64], axis=0, keepdims=True) * (1.0 / HEAD64)
            ms_o = jnp.sum(sq[HEAD64:], axis=0, keepdims=True) * (1.0 / HEAD64)
            ot = ot * lax.rsqrt(jnp.where(low, ms_e, ms_o) + EPS)
            pairs.append(ot.T)
        o_ref[cols, :] = (jnp.concatenate(pairs, axis=-1) * gain).astype(o_ref.dtype)


def _mla_attn(qt, k, vt, out_gain, batch, seq):
    t = k.shape[0]
    return pl.pallas_call(
        _mla_attn_kernel,
        grid=(batch,),
        in_specs=[
            pl.BlockSpec((None, MLA_PAD, seq), lambda bi: (bi, 0, 0)),
            pl.BlockSpec((seq, MLA_PAD), lambda bi: (bi, 0)),
            pl.BlockSpec((None, MLA_PAD, seq), lambda bi: (bi, 0, 0)),
            pl.BlockSpec((1, MLA_WIDTH), lambda bi: (0, 0)),
        ],
        out_specs=pl.BlockSpec((seq, MLA_WIDTH), lambda bi: (bi, 0)),
        out_shape=jax.ShapeDtypeStruct((t, MLA_WIDTH), BF16),
        scratch_shapes=[pltpu.VMEM((MLA_HEADS, seq, TQ), F32)],
        compiler_params=_params("parallel"),
        name="mla_attn",
    )(qt, k, vt, out_gain)


def _out_ffn_kernel(x_ref, ya_ref, yb_ref, yc_ref, woa_ref, wob_ref, woc_ref, g_ref,
                    w1_ref, w2_ref, o_ref, h_ref):
    j = pl.program_id(1)

    @pl.when(j == 0)
    def _():
        xn = (x_ref[...] + _dot(ya_ref[...], woa_ref[...]) + _dot(yb_ref[...], wob_ref[...])
              + _dot(yc_ref[...], woc_ref[...]))
        ms = jnp.mean(xn * xn, axis=-1, keepdims=True)
        h_ref[...] = (xn * lax.rsqrt(ms + EPS) * g_ref[...]).astype(h_ref.dtype)
        o_ref[...] = xn

    a = jnp.maximum(_dot(h_ref[...], w1_ref[...].astype(BF16)), 0.0)
    o_ref[...] += _dot((a * a).astype(BF16), w2_ref[...].astype(BF16))


def _out_ffn(x, ya, yb, yc, w_out_stack, gain2, w1_stack, w2_stack, layer):
    t = x.shape[0]
    wa, wb, wc = ya.shape[1], yb.shape[1], yc.shape[1]
    assert wa == wb and wc == 2 * wa
    row = lambda width: pl.BlockSpec((TM_FFN, width), lambda i, j: (i, 0))
    wo_rows = lambda width, blk: pl.BlockSpec((None, width, D_MODEL), lambda i, j: (layer, blk, 0))
    return pl.pallas_call(
        _out_ffn_kernel,
        grid=(t // TM_FFN, D_FF // TF_FFN),
        in_specs=[
            row(D_MODEL), row(wa), row(wb), row(wc),
            wo_rows(wa, 0), wo_rows(wb, 1), wo_rows(wc, 1),
            pl.BlockSpec((1, D_MODEL), lambda i, j: (0, 0)),
            pl.BlockSpec((None, D_MODEL, TF_FFN), lambda i, j: (layer, 0, j)),
            pl.BlockSpec((None, TF_FFN, D_MODEL), lambda i, j: (layer, j, 0)),
        ],
        out_specs=row(D_MODEL),
        out_shape=jax.ShapeDtypeStruct((t, D_MODEL), F32),
        scratch_shapes=[pltpu.VMEM((TM_FFN, D_MODEL), BF16)],
        compiler_params=_params("parallel", "arbitrary"),
        name="out_ffn",
    )(x, ya, yb, yc, w_out_stack, w_out_stack, w_out_stack, gain2, w1_stack, w2_stack)


def kernel(x, positions, norm1_gain, w_in, gm_v_gain, gm_w_s, gm_b_s, gm_out_gain,
           hg_lower_bound, hg_out_gain, mla_q_a_gain, mla_w_uq, mla_kv_a_gain, mla_w_ukv,
           mla_q_gain, mla_k_gain, mla_out_gain, w_out, norm2_gain, w_ff1, w_ff2):
    batch, seq, _ = x.shape
    t = batch * seq
    depth = w_in.shape[0]
    gm_w = GM_HEADS * HEAD64

    w_in_p = jnp.pad(w_in.astype(BF16), ((0, 0), (0, 0), (0, D_IN_PAD - w_in.shape[2])))

    half = QK_ROPE // 2
    wqt = jnp.swapaxes(mla_w_uq, 1, 2).astype(BF16)
    wkv = mla_w_ukv.astype(BF16).reshape(depth, KV_LORA, MLA_HEADS, QK_NOPE + HEAD64)
    wk = jnp.pad(wkv[..., :QK_NOPE], ((0, 0),) * 3 + ((0, LANES - QK_NOPE),))
    wk = wk.reshape(depth, KV_LORA, MLA_PAD)
    wvt = jnp.swapaxes(wkv[..., QK_NOPE:].reshape(depth, KV_LORA, MLA_WIDTH), 1, 2)
    q_gain_b = jnp.broadcast_to((mla_q_gain * (QK_DIM ** -0.5 * LOG2_E))[:, :, None],
                                (depth, QK_DIM, TM_MLA))
    pad_lanes = ((0, 0), (0, LANES - QK_DIM))
    k_gain_p = jnp.pad(mla_k_gain, pad_lanes)
    k_gain_sw = jnp.pad(jnp.concatenate(
        [mla_k_gain[:, :QK_NOPE], mla_k_gain[:, QK_NOPE + half:], mla_k_gain[:, QK_NOPE:QK_NOPE + half]],
        axis=-1), pad_lanes)

    w_out_b = w_out.astype(BF16)
    bias_full = jnp.broadcast_to(jnp.swapaxes(gm_b_s, 1, 2)[..., None],
                                 (depth, CHUNK, GM_HEADS, HEAD64)).reshape(depth, CHUNK, gm_w)

    inv_freq = ROPE_THETA ** (-jnp.arange(half, dtype=F32) / half)
    cos_t, sin_t, nsin_t = _rope_tables(positions.reshape(1, t), inv_freq[:, None])
    cos_n = jnp.concatenate([jnp.ones((t, QK_NOPE), F32), cos_t.T, cos_t.T,
                             jnp.ones((t, LANES - QK_DIM), F32)], axis=1)
    sin_n = jnp.concatenate([jnp.zeros((t, QK_NOPE), F32), nsin_t.T, sin_t.T,
                             jnp.zeros((t, LANES - QK_DIM), F32)], axis=1)
    tables = (cos_n, sin_n, cos_t, sin_t)

    xs = x.reshape(t, D_MODEL)
    for l in range(depth):
        proj = _in_proj(xs, norm1_gain[l][None, :], w_in_p, l)
        y_a = _gm_mixer(proj, gm_v_gain[l][None, :], gm_w_s[l], bias_full[l],
                        gm_out_gain[l][None, :])
        y_b = _hg_mixer(proj, hg_lower_bound, hg_out_gain[l][None, :], l, batch, seq)
        qt, k, vt = _mla_prep(proj, tables, mla_q_a_gain[l][None, :], mla_kv_a_gain[l][None, :],
                              wqt[l], wk[l], wvt[l], q_gain_b[l], k_gain_p[l][None, :],
                              k_gain_sw[l][None, :], batch, seq)
        y_c = _mla_attn(qt, k, vt, mla_out_gain[l][None, :], batch, seq)
        xs = _out_ffn(xs, y_a, y_b, y_c, w_out_b, norm2_gain[l][None, :], w_ff1, w_ff2, l)
    return xs.reshape(batch, seq, D_MODEL)
```

```python
import functools

import jax
import jax.numpy as jnp
from jax import lax
from jax.experimental import pallas as pl
from jax.experimental.pallas import tpu as pltpu

F32 = jnp.float32
BF16 = jnp.bfloat16

D_MODEL = 1024
DEPTH = 4
CHUNK = 128
EPS = 1e-6
GM_HEADS = 4
HG_HEADS = 4
HEAD64 = 64
PAIR = 2 * HEAD64
MLA_HEADS = 8
QK_NOPE = 64
QK_ROPE = 32
QK_DIM = QK_NOPE + QK_ROPE
Q_LORA = 256
KV_LORA = 128
MLA_WIDTH = MLA_HEADS * HEAD64
ROPE_THETA = 10000.0
LOG2_E = 1.4426950408889634
D_FF = 4 * D_MODEL
LANES = 128
D_IN_PAD = 2048
MLA_PAD = MLA_HEADS * LANES

VMEM_LIMIT = 56 * 1024 * 1024

TM_IN = 1024
TM_GM = 1024
HG_CHUNKS_PER_STEP = 8
TM_MLA = 1024
TQ = 256
TM_FFN = 1024
TF_FFN = 1024


def _params(*sem):
    return pltpu.CompilerParams(dimension_semantics=sem, vmem_limit_bytes=VMEM_LIMIT)


def _dot(a, b):
    return jnp.dot(a, b, preferred_element_type=F32)


def _dot_nt(a, b):
    return lax.dot_general(a, b, (((1,), (1,)), ((), ())), preferred_element_type=F32)


def _dot_tn(a, b):
    return lax.dot_general(a, b, (((0,), (0,)), ((), ())), preferred_element_type=F32)


def _split_dot(x, w_bf16, parts, w_on_left=False):
    acc = None
    rem = x
    for _ in range(parts):
        piece = rem.astype(BF16)
        term = _dot(w_bf16, piece) if w_on_left else _dot(piece, w_bf16)
        acc = term if acc is None else acc + term
        rem = rem - piece.astype(F32)
    return acc


def _head_mean_matrix(width):
    r = lax.broadcasted_iota(jnp.int32, (width, width), 0) // HEAD64
    c = lax.broadcasted_iota(jnp.int32, (width, width), 1) // HEAD64
    return jnp.where(r == c, 1.0 / HEAD64, 0.0).astype(BF16)


def _head_rms(x, gmat):
    ms = _split_dot(x * x, gmat, 2)
    return x * lax.rsqrt(ms + EPS)


def _in_proj_kernel(x_ref, g_ref, w_ref, o_ref):
    x = x_ref[...]
    ms = jnp.mean(x * x, axis=-1, keepdims=True)
    h = x * lax.rsqrt(ms + EPS) * g_ref[...]
    o_ref[...] = _dot(h.astype(BF16), w_ref[...])


def _in_proj(x, gain, w_stack, layer):
    t = x.shape[0]
    return pl.pallas_call(
        _in_proj_kernel,
        grid=(t // TM_IN,),
        in_specs=[
            pl.BlockSpec((TM_IN, D_MODEL), lambda i: (i, 0)),
            pl.BlockSpec((1, D_MODEL), lambda i: (0, 0)),
            pl.BlockSpec((None, D_MODEL, D_IN_PAD), lambda i: (layer, 0, 0)),
        ],
        out_specs=pl.BlockSpec((TM_IN, D_IN_PAD), lambda i: (i, 0)),
        out_shape=jax.ShapeDtypeStruct((t, D_IN_PAD), F32),
        compiler_params=_params("parallel"),
        name="in_proj",
    )(x, gain, w_stack)


def _gm_kernel(u_ref, v_ref, vg_ref, w_ref, b_ref, og_ref, o_ref):
    width = GM_HEADS * HEAD64
    gmat = _head_mean_matrix(width)
    u = jax.nn.gelu(u_ref[...])
    v = _head_rms(jax.nn.gelu(v_ref[...]), gmat) * vg_ref[...]
    row = lax.broadcasted_iota(jnp.int32, (CHUNK, CHUNK), 0)
    col = lax.broadcasted_iota(jnp.int32, (CHUNK, CHUNK), 1)
    causal = row >= col
    lane_head = lax.broadcasted_iota(jnp.int32, (1, PAIR), 1) // HEAD64
    w_tril = [jnp.where(causal, w_ref[h], 0.0).astype(BF16) for h in range(GM_HEADS)]
    w_pair = [jnp.concatenate([w_tril[2 * p], w_tril[2 * p + 1]], axis=1)
              for p in range(GM_HEADS // 2)]
    bias = b_ref[...]
    for c in range(TM_GM // CHUNK):
        rows = slice(c * CHUNK, (c + 1) * CHUNK)
        y_blocks = []
        for p in range(GM_HEADS // 2):
            vb = v[rows, p * PAIR:(p + 1) * PAIR]
            v_cat = jnp.concatenate(
                [jnp.where(lane_head == hh, vb, 0.0).astype(BF16) for hh in range(2)], axis=0)
            y_blocks.append(_dot(w_pair[p], v_cat))
        y = jnp.concatenate(y_blocks, axis=-1) + bias
        out = u[rows] * y
        out = _head_rms(out, gmat) * og_ref[...]
        o_ref[rows, :] = out.astype(o_ref.dtype)


def _gm_mixer(proj, v_gain, w_s, bias_full, out_gain):
    t = proj.shape[0]
    width = GM_HEADS * HEAD64
    return pl.pallas_call(
        _gm_kernel,
        grid=(t // TM_GM,),
        in_specs=[
            pl.BlockSpec((TM_GM, width), lambda i: (i, 0)),
            pl.BlockSpec((TM_GM, width), lambda i: (i, 1)),
            pl.BlockSpec((1, width), lambda i: (0, 0)),
            pl.BlockSpec((GM_HEADS, CHUNK, CHUNK), lambda i: (0, 0, 0)),
            pl.BlockSpec((CHUNK, width), lambda i: (0, 0)),
            pl.BlockSpec((1, width), lambda i: (0, 0)),
        ],
        out_specs=pl.BlockSpec((TM_GM, width), lambda i: (i, 0)),
        out_shape=jax.ShapeDtypeStruct((t, width), BF16),
        compiler_params=_params("parallel"),
        name="gm_mixer",
    )(proj, proj, v_gain, w_s, bias_full, out_gain)


def _anchor_rows(b, m):
    n = b.shape[0]
    if m >= 8:
        pieces = []
        for blk in range(n // (2 * m)):
            a = blk * 2 * m + m - 1
            pieces.append(jnp.broadcast_to(b[a:a + 1, :], (2 * m, b.shape[1])))
        return pieces[0] if len(pieces) == 1 else jnp.concatenate(pieces, axis=0)
    b3 = b.reshape(n // 8, 8, b.shape[1])
    sub = lax.broadcasted_iota(jnp.int32, (1, 8, 1), 1)
    r = None
    for blk in range(8 // (2 * m)):
        a = blk * 2 * m + m - 1
        cand = jnp.broadcast_to(b3[:, a:a + 1, :], b3.shape)
        r = cand if r is None else jnp.where(sub >= blk * 2 * m, cand, r)
    return r.reshape(n, b.shape[1])


def _hg_kernel(layer, q_ref, f_ref, i_ref, g_ref, lb_ref, og_ref, o_ref, st_ref):
    width = HG_HEADS * HEAD64
    n_pairs = HG_HEADS // 2

    @pl.when(pl.program_id(1) == 0)
    def _():
        st_ref[...] = jnp.zeros_like(st_ref)

    lbs = lb_ref[...]
    e = jnp.exp(lbs - jnp.max(lbs, axis=0, keepdims=True))
    soft = e / jnp.sum(e, axis=0, keepdims=True)
    lb = jnp.zeros((1, width), F32)
    for i in range(1, layer + 1):
        lb = lb + soft[i:i + 1, :]

    qr = q_ref[...]
    q_all = qr * jax.nn.sigmoid(qr)
    f_all = lb + (1.0 - lb) * jax.nn.sigmoid(f_ref[...])
    kk_all = 1.0 - f_all
    lf_all = jnp.log(f_all)
    v_all = i_ref[...]

    row = lax.broadcasted_iota(jnp.int32, (CHUNK, CHUNK), 0)
    col = lax.broadcasted_iota(jnp.int32, (CHUNK, CHUNK), 1)
    tri = jnp.where(row >= col, 1.0, 0.0).astype(BF16)
    lane_head = lax.broadcasted_iota(jnp.int32, (1, PAIR), 1) // HEAD64
    t_idx = lax.broadcasted_iota(jnp.int32, (CHUNK, 1), 0)
    rr = lax.broadcasted_iota(jnp.int32, (PAIR, PAIR), 0) // HEAD64
    cc = lax.broadcasted_iota(jnp.int32, (PAIR, PAIR), 1) // HEAD64
    same_head = rr == cc
    code = jnp.where(row >= col, row ^ col, -1)
    level_of = jnp.where(row >= col, 0, -1)
    upper_rows = []
    m = 1
    while m < CHUNK:
        level_of = level_of + jnp.where(code >= m, 1, 0)
        upper_rows.append((t_idx % (2 * m)) >= m)
        m *= 2
    n_levels = len(upper_rows)

    def pair(x, p):
        return x[:, p * PAIR:(p + 1) * PAIR]

    def head_only(xb, hh):
        return jnp.where(lane_head == hh, xb, 0.0).astype(BF16)

    def one_chunk(q_c, kk_c, f_c, lf_c, v_c):
        b_c = _split_dot(lf_c, tri, 3, w_on_left=True)
        o_blocks = []
        for p in range(n_pairs):
            q, kk, f, b, v = (pair(x, p) for x in (q_c, kk_c, f_c, b_c, v_c))
            def both_heads(lhs, rhs):
                stacked = jnp.concatenate([head_only(lhs, 0), head_only(lhs, 1)], axis=0)
                s2 = _dot_nt(stacked, rhs.astype(BF16))
                return s2[:CHUNK], s2[CHUNK:]

            scores = [jnp.where(level_of == 0, s, 0.0) for s in both_heads(q, kk)]
            for lvl in range(n_levels, 0, -1):
                m = 1 << (lvl - 1)
                upper = upper_rows[lvl - 1]
                if m == 1:
                    w = jnp.where(upper, f, 1.0)
                else:
                    d = b - _anchor_rows(b, m)
                    w = jnp.exp(jnp.minimum(d, -d))
                u = jnp.where(upper, q, kk) * w
                for hh, s in enumerate(both_heads(u, u)):
                    scores[hh] = jnp.where(level_of == lvl, s, scores[hh])

            b_last = b[CHUNK - 1:CHUNK, :]
            st = st_ref[p]
            o = _dot_nt((q * jnp.exp(b)).astype(BF16), st.astype(BF16))
            s_cat = jnp.concatenate([scores[0].astype(BF16), scores[1].astype(BF16)], axis=1)
            v_cat = jnp.concatenate([head_only(v, 0), head_only(v, 1)], axis=0)
            o_blocks.append(o + _dot(s_cat, v_cat))
            kd = kk * jnp.exp(jnp.minimum(b_last - b, 0.0))
            upd = _dot_tn(v.astype(BF16), kd.astype(BF16))
            st_ref[p] = st * jnp.exp(b_last) + jnp.where(same_head, upd, 0.0)
        return jnp.concatenate(o_blocks, axis=-1)

    outs = []
    for c in range(HG_CHUNKS_PER_STEP):
        rows = slice(c * CHUNK, (c + 1) * CHUNK)
        outs.append(one_chunk(q_all[rows], kk_all[rows], f_all[rows], lf_all[rows], v_all[rows]))
    o = jnp.concatenate(outs, axis=0)

    gmat = _head_mean_matrix(width)
    gr = g_ref[...]
    out = _head_rms(o, gmat) * og_ref[...] * (gr * jax.nn.sigmoid(gr))
    o_ref[...] = out.astype(o_ref.dtype)


def _hg_mixer(proj, lower_bound_raw, out_gain, layer, batch, seq):
    t = proj.shape[0]
    width = HG_HEADS * HEAD64
    rows = HG_CHUNKS_PER_STEP * CHUNK
    nc = seq // rows

    def col_block(j):
        return pl.BlockSpec((rows, width), lambda bi, ci: (bi * nc + ci, j))

    return pl.pallas_call(
        functools.partial(_hg_kernel, layer),
        grid=(batch, nc),
        in_specs=[
            col_block(2), col_block(3), col_block(4), col_block(5),
            pl.BlockSpec((DEPTH, width), lambda bi, ci: (0, 0)),
            pl.BlockSpec((1, width), lambda bi, ci: (0, 0)),
        ],
        out_specs=pl.BlockSpec((rows, width), lambda bi, ci: (bi * nc + ci, 0)),
        out_shape=jax.ShapeDtypeStruct((t, width), BF16),
        scratch_shapes=[pltpu.VMEM((HG_HEADS // 2, PAIR, PAIR), F32)],
        compiler_params=_params("parallel", "arbitrary"),
        name="hg_mixer",
    )(proj, proj, proj, proj, lower_bound_raw, out_gain)


def _rope_table_kernel(pos_ref, invf_ref, cos_t_ref, sin_t_ref, nsin_t_ref):
    ang_t = invf_ref[...] * pos_ref[...].astype(F32)
    sin_t = jnp.sin(ang_t)
    cos_t_ref[...] = jnp.cos(ang_t)
    sin_t_ref[...] = sin_t
    nsin_t_ref[...] = -sin_t


def _rope_tables(pos_row, invf_col):
    t = pos_row.shape[1]
    half = QK_ROPE // 2
    tm = 2048
    table = pl.BlockSpec((half, tm), lambda i: (0, i))
    return pl.pallas_call(
        _rope_table_kernel,
        grid=(t // tm,),
        in_specs=[pl.BlockSpec((1, tm), lambda i: (0, i)), pl.BlockSpec((half, 1), lambda i: (0, 0))],
        out_specs=[table] * 3,
        out_shape=[jax.ShapeDtypeStruct((half, t), F32)] * 3,
        compiler_params=_params("parallel"),
        name="rope_tables",
    )(pos_row, invf_col)


def _swap_rope_halves(x):
    lane = lax.broadcasted_iota(jnp.int32, (1, LANES), 1)
    half = QK_ROPE // 2
    from_above = pltpu.roll(x, LANES - half, axis=1)
    from_below = pltpu.roll(x, half, axis=1)
    return jnp.where(lane < QK_NOPE + half, from_above, from_below)


def _mla_prep_kernel(cq_ref, ckv_ref, kpe_ref, cos_ref, sin_ref, cos_t_ref, sin_t_ref,
                     qag_ref, kvag_ref, wqt_ref, wk_ref, wvt_ref, qg_ref, kg_ref, kgs_ref,
                     qt_out, k_out, vt_out):
    def rms(x, gain):
        ms = jnp.mean(x * x, axis=-1, keepdims=True)
        return x * lax.rsqrt(ms + EPS) * gain

    half = QK_ROPE // 2
    hq = rms(cq_ref[...], qag_ref[...]).astype(BF16)
    hkv = rms(ckv_ref[...], kvag_ref[...]).astype(BF16)

    qt = _dot_nt(wqt_ref[...], hq)
    cos_t = cos_t_ref[...]
    sin_t = sin_t_ref[...]
    for h in range(MLA_HEADS):
        x = qt[h * QK_DIM:(h + 1) * QK_DIM, :]
        ms = jnp.sum(x * x, axis=0, keepdims=True) * (1.0 / QK_DIM)
        xn = x * lax.rsqrt(ms + EPS) * qg_ref[...]
        r1 = xn[QK_NOPE:QK_NOPE + half]
        r2 = xn[QK_NOPE + half:]
        base = h * LANES
        qt_out[base:base + QK_NOPE, :] = xn[:QK_NOPE].astype(qt_out.dtype)
        qt_out[base + QK_NOPE:base + QK_NOPE + half, :] = (r1 * cos_t - r2 * sin_t).astype(qt_out.dtype)
        qt_out[base + QK_NOPE + half:base + QK_DIM, :] = (r2 * cos_t + r1 * sin_t).astype(qt_out.dtype)
        qt_out[base + QK_DIM:base + LANES, :] = jnp.zeros((LANES - QK_DIM, x.shape[1]), qt_out.dtype)

    vt = _dot_nt(wvt_ref[...], hkv).astype(vt_out.dtype)
    ones = jnp.ones((HEAD64, vt.shape[1]), vt_out.dtype)
    for h in range(MLA_HEADS):
        vals = vt[h * HEAD64:(h + 1) * HEAD64, :]
        lo, hi = (vals, ones) if h % 2 == 0 else (ones, vals)
        vt_out[h * LANES:h * LANES + HEAD64, :] = lo
        vt_out[h * LANES + HEAD64:(h + 1) * LANES, :] = hi

    kn = _dot(hkv, wk_ref[...])
    kpe = pltpu.roll(kpe_ref[...], QK_NOPE, axis=1)
    gain_cos = kg_ref[...] * cos_ref[...]
    rot = _swap_rope_halves(kpe) * kgs_ref[...] * sin_ref[...]
    for h in range(MLA_HEADS):
        lanes = slice(h * LANES, (h + 1) * LANES)
        x = kn[:, lanes] + kpe
        ms = jnp.sum(x * x, axis=-1, keepdims=True) * (1.0 / QK_DIM)
        k_out[:, lanes] = (lax.rsqrt(ms + EPS) * (x * gain_cos + rot)).astype(k_out.dtype)


def _mla_prep(proj, tables, qa_gain, kva_gain, wqt, wk, wvt, q_gain_b, k_gain, k_gain_sw,
              batch, seq):
    t = proj.shape[0]
    half = QK_ROPE // 2
    tiles_per_seq = seq // TM_MLA
    cos_n, sin_n, cos_t, sin_t = tables
    full = lambda shape: pl.BlockSpec(shape, lambda i: (0,) * len(shape))
    out = jax.ShapeDtypeStruct((t, MLA_PAD), BF16)
    out_t = jax.ShapeDtypeStruct((batch, MLA_PAD, seq), BF16)
    spec_t = pl.BlockSpec((None, MLA_PAD, TM_MLA),
                          lambda i: (i // tiles_per_seq, 0, i % tiles_per_seq))
    return pl.pallas_call(
        _mla_prep_kernel,
        grid=(t // TM_MLA,),
        in_specs=[
            pl.BlockSpec((TM_MLA, Q_LORA), lambda i: (i, 6)),
            pl.BlockSpec((TM_MLA, KV_LORA), lambda i: (i, 14)),
            pl.BlockSpec((TM_MLA, LANES), lambda i: (i, 15)),
            pl.BlockSpec((TM_MLA, LANES), lambda i: (i, 0)),
            pl.BlockSpec((TM_MLA, LANES), lambda i: (i, 0)),
            pl.BlockSpec((half, TM_MLA), lambda i: (0, i)),
            pl.BlockSpec((half, TM_MLA), lambda i: (0, i)),
            full((1, Q_LORA)), full((1, KV_LORA)),
            full((MLA_HEADS * QK_DIM, Q_LORA)), full((KV_LORA, MLA_PAD)),
            full((MLA_WIDTH, KV_LORA)),
            full((QK_DIM, TM_MLA)), full((1, LANES)), full((1, LANES)),
        ],
        out_specs=[spec_t, pl.BlockSpec((TM_MLA, MLA_PAD), lambda i: (i, 0)), spec_t],
        out_shape=[out_t, out, out_t],
        compiler_params=_params("parallel"),
        name="mla_prep",
    )(proj, proj, proj, cos_n, sin_n, cos_t, sin_t, qa_gain, kva_gain, wqt, wk, wvt,
      q_gain_b, k_gain, k_gain_sw)


def _mla_attn_kernel(qt_ref, k_ref, vt_ref, og_ref, o_ref, st_ref):
    seq = k_ref.shape[0]
    row = lax.broadcasted_iota(jnp.int32, (TQ, TQ), 0)
    col = lax.broadcasted_iota(jnp.int32, (TQ, TQ), 1)
    causal_t = col >= row
    low = lax.broadcasted_iota(jnp.int32, (LANES, 1), 0) < HEAD64
    gain = og_ref[...]

    for qi in range(seq // TQ):
        cols = slice(qi * TQ, (qi + 1) * TQ)
        n_keys = (qi + 1) * TQ
        spans = [(s, min(2 * TQ, n_keys - s)) for s in range(0, n_keys, 2 * TQ)]
        maxes = [None] * MLA_HEADS
        for start, rows in spans:
            for h in range(MLA_HEADS):
                lanes = slice(h * LANES, (h + 1) * LANES)
                st = _dot(k_ref[start:start + rows, lanes], qt_ref[lanes, cols])
                if start + rows == n_keys:
                    last = jnp.where(causal_t, st[rows - TQ:], -jnp.inf)
                    st = last if rows == TQ else jnp.concatenate([st[:rows - TQ], last], axis=0)
                st_ref[h, start:start + rows, :] = st
                cm = jnp.max(st, axis=0, keepdims=True)
                maxes[h] = cm if maxes[h] is None else jnp.maximum(maxes[h], cm)
        accs = []
        for h in range(MLA_HEADS):
            lanes = slice(h * LANES, (h + 1) * LANES)
            acc = None
            for start, rows in spans:
                x = st_ref[h, start:start + rows, :] - maxes[h]
                pt = jnp.exp2(x.astype(BF16))
                part = _dot(vt_ref[lanes, start:start + rows], pt)
                acc = part if acc is None else acc + part
            accs.append(acc)
        pairs = []
        for p in range(MLA_HEADS // 2):
            acc_e = accs[2 * p]
            acc_o = accs[2 * p + 1]
            ot = jnp.where(low, acc_e / acc_e[HEAD64:HEAD64 + 1, :], acc_o / acc_o[0:1, :])
            sq = ot * ot
            ms_e = jnp.sum(sq[:HEAD64], axis=0, keepdims=True) * (1.0 / HEAD64)
            ms_o = jnp.sum(sq[HEAD64:], axis=0, keepdims=True) * (1.0 / HEAD64)
            ot = ot * lax.rsqrt(jnp.where(low, ms_e, ms_o) + EPS)
            pairs.append(ot.T)
        o_ref[cols, :] = (jnp.concatenate(pairs, axis=-1) * gain).astype(o_ref.dtype)


def _mla_attn(qt, k, vt, out_gain, batch, seq):
    t = k.shape[0]
    return pl.pallas_call(
        _mla_attn_kernel,
        grid=(batch,),
        in_specs=[
            pl.BlockSpec((None, MLA_PAD, seq), lambda bi: (bi, 0, 0)),
            pl.BlockSpec((seq, MLA_PAD), lambda bi: (bi, 0)),
            pl.BlockSpec((None, MLA_PAD, seq), lambda bi: (bi, 0, 0)),
            pl.BlockSpec((1, MLA_WIDTH), lambda bi: (0, 0)),
        ],
        out_specs=pl.BlockSpec((seq, MLA_WIDTH), lambda bi: (bi, 0)),
        out_shape=jax.ShapeDtypeStruct((t, MLA_WIDTH), BF16),
        scratch_shapes=[pltpu.VMEM((MLA_HEADS, seq, TQ), F32)],
        compiler_params=_params("parallel"),
        name="mla_attn",
    )(qt, k, vt, out_gain)


def _out_ffn_kernel(x_ref, ya_ref, yb_ref, yc_ref, woa_ref, wob_ref, woc_ref, g_ref,
                    w1_ref, w2_ref, o_ref, h_ref):
    j = pl.program_id(1)

    @pl.when(j == 0)
    def _():
        xn = (x_ref[...] + _dot(ya_ref[...], woa_ref[...]) + _dot(yb_ref[...], wob_ref[...])
              + _dot(yc_ref[...], woc_ref[...]))
        ms = jnp.mean(xn * xn, axis=-1, keepdims=True)
        h_ref[...] = (xn * lax.rsqrt(ms + EPS) * g_ref[...]).astype(h_ref.dtype)
        o_ref[...] = xn

    a = jnp.maximum(_dot(h_ref[...], w1_ref[...].astype(BF16)), 0.0)
    o_ref[...] += _dot((a * a).astype(BF16), w2_ref[...].astype(BF16))


def _out_ffn(x, ya, yb, yc, w_out_stack, gain2, w1_stack, w2_stack, layer):
    t = x.shape[0]
    wa, wb, wc = ya.shape[1], yb.shape[1], yc.shape[1]
    assert wa == wb and wc == 2 * wa
    row = lambda width: pl.BlockSpec((TM_FFN, width), lambda i, j: (i, 0))
    wo_rows = lambda width, blk: pl.BlockSpec((None, width, D_MODEL), lambda i, j: (layer, blk, 0))
    return pl.pallas_call(
        _out_ffn_kernel,
        grid=(t // TM_FFN, D_FF // TF_FFN),
        in_specs=[
            row(D_MODEL), row(wa), row(wb), row(wc),
            wo_rows(wa, 0), wo_rows(wb, 1), wo_rows(wc, 1),
            pl.BlockSpec((1, D_MODEL), lambda i, j: (0, 0)),
            pl.BlockSpec((None, D_MODEL, TF_FFN), lambda i, j: (layer, 0, j)),
            pl.BlockSpec((None, TF_FFN, D_MODEL), lambda i, j: (layer, j, 0)),
        ],
        out_specs=row(D_MODEL),
        out_shape=jax.ShapeDtypeStruct((t, D_MODEL), F32),
        scratch_shapes=[pltpu.VMEM((TM_FFN, D_MODEL), BF16)],
        compiler_params=_params("parallel", "arbitrary"),
        name="out_ffn",
    )(x, ya, yb, yc, w_out_stack, w_out_stack, w_out_stack, gain2, w1_stack, w2_stack)


def kernel(x, positions, norm1_gain, w_in, gm_v_gain, gm_w_s, gm_b_s, gm_out_gain,
           hg_lower_bound, hg_out_gain, mla_q_a_gain, mla_w_uq, mla_kv_a_gain, mla_w_ukv,
           mla_q_gain, mla_k_gain, mla_out_gain, w_out, norm2_gain, w_ff1, w_ff2):
    batch, seq, _ = x.shape
    t = batch * seq
    depth = w_in.shape[0]
    gm_w = GM_HEADS * HEAD64

    w_in_p = jnp.pad(w_in.astype(BF16), ((0, 0), (0, 0), (0, D_IN_PAD - w_in.shape[2])))

    half = QK_ROPE // 2
    wqt = jnp.swapaxes(mla_w_uq, 1, 2).astype(BF16)
    wkv = mla_w_ukv.astype(BF16).reshape(depth, KV_LORA, MLA_HEADS, QK_NOPE + HEAD64)
    wk = jnp.pad(wkv[..., :QK_NOPE], ((0, 0),) * 3 + ((0, LANES - QK_NOPE),))
    wk = wk.reshape(depth, KV_LORA, MLA_PAD)
    wvt = jnp.swapaxes(wkv[..., QK_NOPE:].reshape(depth, KV_LORA, MLA_WIDTH), 1, 2)
    q_gain_b = jnp.broadcast_to((mla_q_gain * (QK_DIM ** -0.5 * LOG2_E))[:, :, None],
                                (depth, QK_DIM, TM_MLA))
    pad_lanes = ((0, 0), (0, LANES - QK_DIM))
    k_gain_p = jnp.pad(mla_k_gain, pad_lanes)
    k_gain_sw = jnp.pad(jnp.concatenate(
        [mla_k_gain[:, :QK_NOPE], mla_k_gain[:, QK_NOPE + half:], mla_k_gain[:, QK_NOPE:QK_NOPE + half]],
        axis=-1), pad_lanes)

    w_out_b = w_out.astype(BF16)
    bias_full = jnp.broadcast_to(jnp.swapaxes(gm_b_s, 1, 2)[..., None],
                                 (depth, CHUNK, GM_HEADS, HEAD64)).reshape(depth, CHUNK, gm_w)

    inv_freq = ROPE_THETA ** (-jnp.arange(half, dtype=F32) / half)
    cos_t, sin_t, nsin_t = _rope_tables(positions.reshape(1, t), inv_freq[:, None])
    cos_n = jnp.concatenate([jnp.ones((t, QK_NOPE), F32), cos_t.T, cos_t.T,
                             jnp.ones((t, LANES - QK_DIM), F32)], axis=1)
    sin_n = jnp.concatenate([jnp.zeros((t, QK_NOPE), F32), nsin_t.T, sin_t.T,
                             jnp.zeros((t, LANES - QK_DIM), F32)], axis=1)
    tables = (cos_n, sin_n, cos_t, sin_t)

    xs = x.reshape(t, D_MODEL)
    for l in range(depth):
        proj = _in_proj(xs, norm1_gain[l][None, :], w_in_p, l)
        y_a = _gm_mixer(proj, gm_v_gain[l][None, :], gm_w_s[l], bias_full[l],
                        gm_out_gain[l][None, :])
        y_b = _hg_mixer(proj, hg_lower_bound, hg_out_gain[l][None, :], l, batch, seq)
        qt, k, vt = _mla_prep(proj, tables, mla_q_a_gain[l][None, :], mla_kv_a_gain[l][None, :],
                              wqt[l], wk[l], wvt[l], q_gain_b[l], k_gain_p[l][None, :],
                              k_gain_sw[l][None, :], batch, seq)
        y_c = _mla_attn(qt, k, vt, mla_out_gain[l][None, :], batch, seq)
        xs = _out_ffn(xs, y_a, y_b, y_c, w_out_b, norm2_gain[l][None, :], w_ff1, w_ff2, l)
    return xs.reshape(batch, seq, D_MODEL)
```

```python
import functools

import jax
import jax.numpy as jnp
from jax import lax
from jax.experimental import pallas as pl
from jax.experimental.pallas import tpu as pltpu

F32 = jnp.float32
BF16 = jnp.bfloat16

D_MODEL = 1024
DEPTH = 4
CHUNK = 128
EPS = 1e-6
GM_HEADS = 4
HG_HEADS = 4
HEAD64 = 64
PAIR = 2 * HEAD64
MLA_HEADS = 8
QK_NOPE = 64
QK_ROPE = 32
QK_DIM = QK_NOPE + QK_ROPE
Q_LORA = 256
KV_LORA = 128
MLA_WIDTH = MLA_HEADS * HEAD64
ROPE_THETA = 10000.0
LOG2_E = 1.4426950408889634
D_FF = 4 * D_MODEL
LANES = 128
D_IN_PAD = 2048
MLA_PAD = MLA_HEADS * LANES

VMEM_LIMIT = 56 * 1024 * 1024

TM_IN = 1024
TM_GM = 2048
HG_CHUNKS_PER_STEP = 16
TM_MLA = 2048
TQ = 256
TM_FFN = 1024
TF_FFN = 1024


def _params(*sem):
    return pltpu.CompilerParams(dimension_semantics=sem, vmem_limit_bytes=VMEM_LIMIT)


def _dot(a, b):
    return jnp.dot(a, b, preferred_element_type=F32)


def _dot_nt(a, b):
    return lax.dot_general(a, b, (((1,), (1,)), ((), ())), preferred_element_type=F32)


def _dot_tn(a, b):
    return lax.dot_general(a, b, (((0,), (0,)), ((), ())), preferred_element_type=F32)


def _split_dot(x, w_bf16, parts, w_on_left=False):
    acc = None
    rem = x
    for _ in range(parts):
        piece = rem.astype(BF16)
        term = _dot(w_bf16, piece) if w_on_left else _dot(piece, w_bf16)
        acc = term if acc is None else acc + term
        rem = rem - piece.astype(F32)
    return acc


def _head_mean_matrix(width):
    r = lax.broadcasted_iota(jnp.int32, (width, width), 0) // HEAD64
    c = lax.broadcasted_iota(jnp.int32, (width, width), 1) // HEAD64
    return jnp.where(r == c, 1.0 / HEAD64, 0.0).astype(BF16)


def _head_rms(x, gmat):
    ms = _split_dot(x * x, gmat, 2)
    return x * lax.rsqrt(ms + EPS)


def _in_proj_kernel(x_ref, g_ref, w_ref, o_ref):
    x = x_ref[...]
    ms = jnp.mean(x * x, axis=-1, keepdims=True)
    h = x * lax.rsqrt(ms + EPS) * g_ref[...]
    o_ref[...] = _dot(h.astype(BF16), w_ref[...])


def _in_proj(x, gain, w_stack, layer):
    t = x.shape[0]
    return pl.pallas_call(
        _in_proj_kernel,
        grid=(t // TM_IN,),
        in_specs=[
            pl.BlockSpec((TM_IN, D_MODEL), lambda i: (i, 0)),
            pl.BlockSpec((1, D_MODEL), lambda i: (0, 0)),
            pl.BlockSpec((None, D_MODEL, D_IN_PAD), lambda i: (layer, 0, 0)),
        ],
        out_specs=pl.BlockSpec((TM_IN, D_IN_PAD), lambda i: (i, 0)),
        out_shape=jax.ShapeDtypeStruct((t, D_IN_PAD), F32),
        compiler_params=_params("parallel"),
        name="in_proj",
    )(x, gain, w_stack)


def _gm_kernel(u_ref, v_ref, vg_ref, w_ref, b_ref, og_ref, o_ref):
    width = GM_HEADS * HEAD64
    gmat = _head_mean_matrix(width)
    u = jax.nn.gelu(u_ref[...])
    v = _head_rms(jax.nn.gelu(v_ref[...]), gmat) * vg_ref[...]
    row = lax.broadcasted_iota(jnp.int32, (CHUNK, CHUNK), 0)
    col = lax.broadcasted_iota(jnp.int32, (CHUNK, CHUNK), 1)
    causal = row >= col
    lane_head = lax.broadcasted_iota(jnp.int32, (1, PAIR), 1) // HEAD64
    w_tril = [jnp.where(causal, w_ref[h], 0.0).astype(BF16) for h in range(GM_HEADS)]
    w_pair = [jnp.concatenate([w_tril[2 * p], w_tril[2 * p + 1]], axis=1)
              for p in range(GM_HEADS // 2)]
    bias = b_ref[...]
    for c in range(TM_GM // CHUNK):
        rows = slice(c * CHUNK, (c + 1) * CHUNK)
        y_blocks = []
        for p in range(GM_HEADS // 2):
            vb = v[rows, p * PAIR:(p + 1) * PAIR]
            v_cat = jnp.concatenate(
                [jnp.where(lane_head == hh, vb, 0.0).astype(BF16) for hh in range(2)], axis=0)
            y_blocks.append(_dot(w_pair[p], v_cat))
        y = jnp.concatenate(y_blocks, axis=-1) + bias
        out = u[rows] * y
        out = _head_rms(out, gmat) * og_ref[...]
        o_ref[rows, :] = out.astype(o_ref.dtype)


def _gm_mixer(proj, v_gain, w_s, bias_full, out_gain):
    t = proj.shape[0]
    width = GM_HEADS * HEAD64
    return pl.pallas_call(
        _gm_kernel,
        grid=(t // TM_GM,),
        in_specs=[
            pl.BlockSpec((TM_GM, width), lambda i: (i, 0)),
            pl.BlockSpec((TM_GM, width), lambda i: (i, 1)),
            pl.BlockSpec((1, width), lambda i: (0, 0)),
            pl.BlockSpec((GM_HEADS, CHUNK, CHUNK), lambda i: (0, 0, 0)),
            pl.BlockSpec((CHUNK, width), lambda i: (0, 0)),
            pl.BlockSpec((1, width), lambda i: (0, 0)),
        ],
        out_specs=pl.BlockSpec((TM_GM, width), lambda i: (i, 0)),
        out_shape=jax.ShapeDtypeStruct((t, width), BF16),
        compiler_params=_params("parallel"),
        name="gm_mixer",
    )(proj, proj, v_gain, w_s, bias_full, out_gain)


def _anchor_rows(b, m):
    n = b.shape[0]
    if m >= 8:
        pieces = []
        for blk in range(n // (2 * m)):
            a = blk * 2 * m + m - 1
            pieces.append(jnp.broadcast_to(b[a:a + 1, :], (2 * m, b.shape[1])))
        return pieces[0] if len(pieces) == 1 else jnp.concatenate(pieces, axis=0)
    b3 = b.reshape(n // 8, 8, b.shape[1])
    sub = lax.broadcasted_iota(jnp.int32, (1, 8, 1), 1)
    r = None
    for blk in range(8 // (2 * m)):
        a = blk * 2 * m + m - 1
        cand = jnp.broadcast_to(b3[:, a:a + 1, :], b3.shape)
        r = cand if r is None else jnp.where(sub >= blk * 2 * m, cand, r)
    return r.reshape(n, b.shape[1])


def _hg_kernel(layer, q_ref, f_ref, i_ref, g_ref, lb_ref, og_ref, o_ref, st_ref):
    width = HG_HEADS * HEAD64
    n_pairs = HG_HEADS // 2

    @pl.when(pl.program_id(1) == 0)
    def _():
        st_ref[...] = jnp.zeros_like(st_ref)

    lbs = lb_ref[...]
    e = jnp.exp(lbs - jnp.max(lbs, axis=0, keepdims=True))
    soft = e / jnp.sum(e, axis=0, keepdims=True)
    lb = jnp.zeros((1, width), F32)
    for i in range(1, layer + 1):
        lb = lb + soft[i:i + 1, :]

    qr = q_ref[...]
    q_all = qr * jax.nn.sigmoid(qr)
    f_all = lb + (1.0 - lb) * jax.nn.sigmoid(f_ref[...])
    kk_all = 1.0 - f_all
    lf_all = jnp.log(f_all)
    v_all = i_ref[...]

    row = lax.broadcasted_iota(jnp.int32, (CHUNK, CHUNK), 0)
    col = lax.broadcasted_iota(jnp.int32, (CHUNK, CHUNK), 1)
    tri = jnp.where(row >= col, 1.0, 0.0).astype(BF16)
    lane_head = lax.broadcasted_iota(jnp.int32, (1, PAIR), 1) // HEAD64
    t_idx = lax.broadcasted_iota(jnp.int32, (CHUNK, 1), 0)
    rr = lax.broadcasted_iota(jnp.int32, (PAIR, PAIR), 0) // HEAD64
    cc = lax.broadcasted_iota(jnp.int32, (PAIR, PAIR), 1) // HEAD64
    same_head = rr == cc
    code = jnp.where(row >= col, row ^ col, -1)
    level_of = jnp.where(row >= col, 0, -1)
    upper_rows = []
    m = 1
    while m < CHUNK:
        level_of = level_of + jnp.where(code >= m, 1, 0)
        upper_rows.append((t_idx % (2 * m)) >= m)
        m *= 2
    n_levels = len(upper_rows)

    def pair(x, p):
        return x[:, p * PAIR:(p + 1) * PAIR]

    def head_only(xb, hh):
        return jnp.where(lane_head == hh, xb, 0.0).astype(BF16)

    def one_chunk(q_c, kk_c, f_c, lf_c, v_c):
        b_c = _split_dot(lf_c, tri, 3, w_on_left=True)
        o_blocks = []
        for p in range(n_pairs):
            q, kk, f, b, v = (pair(x, p) for x in (q_c, kk_c, f_c, b_c, v_c))
            def both_heads(lhs, rhs):
                stacked = jnp.concatenate([head_only(lhs, 0), head_only(lhs, 1)], axis=0)
                s2 = _dot_nt(stacked, rhs.astype(BF16))
                return s2[:CHUNK], s2[CHUNK:]

            scores = [jnp.where(level_of == 0, s, 0.0) for s in both_heads(q, kk)]
            for lvl in range(n_levels, 0, -1):
                m = 1 << (lvl - 1)
                upper = upper_rows[lvl - 1]
                if m == 1:
                    w = jnp.where(upper, f, 1.0)
                else:
                    d = b - _anchor_rows(b, m)
                    w = jnp.exp(jnp.minimum(d, -d))
                u = jnp.where(upper, q, kk) * w
                for hh, s in enumerate(both_heads(u, u)):
                    scores[hh] = jnp.where(level_of == lvl, s, scores[hh])

            b_last = b[CHUNK - 1:CHUNK, :]
            st = st_ref[p]
            o = _dot_nt((q * jnp.exp(b)).astype(BF16), st.astype(BF16))
            s_cat = jnp.concatenate([scores[0].astype(BF16), scores[1].astype(BF16)], axis=1)
            v_cat = jnp.concatenate([head_only(v, 0), head_only(v, 1)], axis=0)
            o_blocks.append(o + _dot(s_cat, v_cat))
            kd = kk * jnp.exp(jnp.minimum(b_last - b, 0.0))
            upd = _dot_tn(v.astype(BF16), kd.astype(BF16))
            st_ref[p] = st * jnp.exp(b_last) + jnp.where(same_head, upd, 0.0)
        return jnp.concatenate(o_blocks, axis=-1)

    outs = []
    for c in range(HG_CHUNKS_PER_STEP):
        rows = slice(c * CHUNK, (c + 1) * CHUNK)
        outs.append(one_chunk(q_all[rows], kk_all[rows], f_all[rows], lf_all[rows], v_all[rows]))
    o = jnp.concatenate(outs, axis=0)

    gmat = _head_mean_matrix(width)
    gr = g_ref[...]
    out = _head_rms(o, gmat) * og_ref[...] * (gr * jax.nn.sigmoid(gr))
    o_ref[...] = out.astype(o_ref.dtype)


def _hg_mixer(proj, lower_bound_raw, out_gain, layer, batch, seq):
    t = proj.shape[0]
    width = HG_HEADS * HEAD64
    rows = HG_CHUNKS_PER_STEP * CHUNK
    nc = seq // rows

    def col_block(j):
        return pl.BlockSpec((rows, width), lambda bi, ci: (bi * nc + ci, j))

    return pl.pallas_call(
        functools.partial(_hg_kernel, layer),
        grid=(batch, nc),
        in_specs=[
            col_block(2), col_block(3), col_block(4), col_block(5),
            pl.BlockSpec((DEPTH, width), lambda bi, ci: (0, 0)),
            pl.BlockSpec((1, width), lambda bi, ci: (0, 0)),
        ],
        out_specs=pl.BlockSpec((rows, width), lambda bi, ci: (bi * nc + ci, 0)),
        out_shape=jax.ShapeDtypeStruct((t, width), BF16),
        scratch_shapes=[pltpu.VMEM((HG_HEADS // 2, PAIR, PAIR), F32)],
        compiler_params=_params("parallel", "arbitrary"),
        name="hg_mixer",
    )(proj, proj, proj, proj, lower_bound_raw, out_gain)


def _rope_table_kernel(pos_ref, invf_ref, cos_t_ref, sin_t_ref, nsin_t_ref):
    ang_t = invf_ref[...] * pos_ref[...].astype(F32)
    sin_t = jnp.sin(ang_t)
    cos_t_ref[...] = jnp.cos(ang_t)
    sin_t_ref[...] = sin_t
    nsin_t_ref[...] = -sin_t


def _rope_tables(pos_row, invf_col):
    t = pos_row.shape[1]
    half = QK_ROPE // 2
    tm = 2048
    table = pl.BlockSpec((half, tm), lambda i: (0, i))
    return pl.pallas_call(
        _rope_table_kernel,
        grid=(t // tm,),
        in_specs=[pl.BlockSpec((1, tm), lambda i: (0, i)), pl.BlockSpec((half, 1), lambda i: (0, 0))],
        out_specs=[table] * 3,
        out_shape=[jax.ShapeDtypeStruct((half, t), F32)] * 3,
        compiler_params=_params("parallel"),
        name="rope_tables",
    )(pos_row, invf_col)


def _swap_rope_halves(x):
    lane = lax.broadcasted_iota(jnp.int32, (1, LANES), 1)
    half = QK_ROPE // 2
    from_above = pltpu.roll(x, LANES - half, axis=1)
    from_below = pltpu.roll(x, half, axis=1)
    return jnp.where(lane < QK_NOPE + half, from_above, from_below)


def _mla_prep_kernel(cq_ref, ckv_ref, kpe_ref, cos_ref, sin_ref, cos_t_ref, sin_t_ref,
                     qag_ref, kvag_ref, wqt_ref, wk_ref, wvt_ref, qg_ref, kg_ref, kgs_ref,
                     qt_out, k_out, vt_out):
    def rms(x, gain):
        ms = jnp.mean(x * x, axis=-1, keepdims=True)
        return x * lax.rsqrt(ms + EPS) * gain

    half = QK_ROPE // 2
    hq = rms(cq_ref[...], qag_ref[...]).astype(BF16)
    hkv = rms(ckv_ref[...], kvag_ref[...]).astype(BF16)

    qt = _dot_nt(wqt_ref[...], hq)
    cos_t = cos_t_ref[...]
    sin_t = sin_t_ref[...]
    for h in range(MLA_HEADS):
        x = qt[h * QK_DIM:(h + 1) * QK_DIM, :]
        ms = jnp.sum(x * x, axis=0, keepdims=True) * (1.0 / QK_DIM)
        xn = x * lax.rsqrt(ms + EPS) * qg_ref[...]
        r1 = xn[QK_NOPE:QK_NOPE + half]
        r2 = xn[QK_NOPE + half:]
        base = h * LANES
        qt_out[base:base + QK_NOPE, :] = xn[:QK_NOPE].astype(qt_out.dtype)
        qt_out[base + QK_NOPE:base + QK_NOPE + half, :] = (r1 * cos_t - r2 * sin_t).astype(qt_out.dtype)
        qt_out[base + QK_NOPE + half:base + QK_DIM, :] = (r2 * cos_t + r1 * sin_t).astype(qt_out.dtype)
        qt_out[base + QK_DIM:base + LANES, :] = jnp.zeros((LANES - QK_DIM, x.shape[1]), qt_out.dtype)

    vt = _dot_nt(wvt_ref[...], hkv).astype(vt_out.dtype)
    ones = jnp.ones((HEAD64, vt.shape[1]), vt_out.dtype)
    for h in range(MLA_HEADS):
        vals = vt[h * HEAD64:(h + 1) * HEAD64, :]
        lo, hi = (vals, ones) if h % 2 == 0 else (ones, vals)
        vt_out[h * LANES:h * LANES + HEAD64, :] = lo
        vt_out[h * LANES + HEAD64:(h + 1) * LANES, :] = hi

    kn = _dot(hkv, wk_ref[...])
    kpe = pltpu.roll(kpe_ref[...], QK_NOPE, axis=1)
    gain_cos = kg_ref[...] * cos_ref[...]
    rot = _swap_rope_halves(kpe) * kgs_ref[...] * sin_ref[...]
    for h in range(MLA_HEADS):
        lanes = slice(h * LANES, (h + 1) * LANES)
        x = kn[:, lanes] + kpe
        ms = jnp.sum(x * x, axis=-1, keepdims=True) * (1.0 / QK_DIM)
        k_out[:, lanes] = (lax.rsqrt(ms + EPS) * (x * gain_cos + rot)).astype(k_out.dtype)


def _mla_prep(proj, tables, qa_gain, kva_gain, wqt, wk, wvt, q_gain_b, k_gain, k_gain_sw,
              batch, seq):
    t = proj.shape[0]
    half = QK_ROPE // 2
    tiles_per_seq = seq // TM_MLA
    cos_n, sin_n, cos_t, sin_t = tables
    full = lambda shape: pl.BlockSpec(shape, lambda i: (0,) * len(shape))
    out = jax.ShapeDtypeStruct((t, MLA_PAD), BF16)
    out_t = jax.ShapeDtypeStruct((batch, MLA_PAD, seq), BF16)
    spec_t = pl.BlockSpec((None, MLA_PAD, TM_MLA),
                          lambda i: (i // tiles_per_seq, 0, i % tiles_per_seq))
    return pl.pallas_call(
        _mla_prep_kernel,
        grid=(t // TM_MLA,),
        in_specs=[
            pl.BlockSpec((TM_MLA, Q_LORA), lambda i: (i, 6)),
            pl.BlockSpec((TM_MLA, KV_LORA), lambda i: (i, 14)),
            pl.BlockSpec((TM_MLA, LANES), lambda i: (i, 15)),
            pl.BlockSpec((TM_MLA, LANES), lambda i: (i, 0)),
            pl.BlockSpec((TM_MLA, LANES), lambda i: (i, 0)),
            pl.BlockSpec((half, TM_MLA), lambda i: (0, i)),
            pl.BlockSpec((half, TM_MLA), lambda i: (0, i)),
            full((1, Q_LORA)), full((1, KV_LORA)),
            full((MLA_HEADS * QK_DIM, Q_LORA)), full((KV_LORA, MLA_PAD)),
            full((MLA_WIDTH, KV_LORA)),
            full((QK_DIM, TM_MLA)), full((1, LANES)), full((1, LANES)),
        ],
        out_specs=[spec_t, pl.BlockSpec((TM_MLA, MLA_PAD), lambda i: (i, 0)), spec_t],
        out_shape=[out_t, out, out_t],
        compiler_params=_params("parallel"),
        name="mla_prep",
    )(proj, proj, proj, cos_n, sin_n, cos_t, sin_t, qa_gain, kva_gain, wqt, wk, wvt,
      q_gain_b, k_gain, k_gain_sw)


def _mla_attn_kernel(qt_ref, k_ref, vt_ref, og_ref, o_ref, st_ref):
    seq = k_ref.shape[0]
    row = lax.broadcasted_iota(jnp.int32, (TQ, TQ), 0)
    col = lax.broadcasted_iota(jnp.int32, (TQ, TQ), 1)
    causal_t = col >= row
    low = lax.broadcasted_iota(jnp.int32, (LANES, 1), 0) < HEAD64
    gain = og_ref[...]

    for qi in range(seq // TQ):
        cols = slice(qi * TQ, (qi + 1) * TQ)
        n_keys = (qi + 1) * TQ
        spans = [(s, min(2 * TQ, n_keys - s)) for s in range(0, n_keys, 2 * TQ)]
        maxes = [None] * MLA_HEADS
        for start, rows in spans:
            for h in range(MLA_HEADS):
                lanes = slice(h * LANES, (h + 1) * LANES)
                st = _dot(k_ref[start:start + rows, lanes], qt_ref[lanes, cols])
                if start + rows == n_keys:
                    last = jnp.where(causal_t, st[rows - TQ:], -jnp.inf)
                    st = last if rows == TQ else jnp.concatenate([st[:rows - TQ], last], axis=0)
                st_ref[h, start:start + rows, :] = st
                cm = jnp.max(st, axis=0, keepdims=True)
                maxes[h] = cm if maxes[h] is None else jnp.maximum(maxes[h], cm)
        accs = []
        for h in range(MLA_HEADS):
            lanes = slice(h * LANES, (h + 1) * LANES)
            acc = None
            for start, rows in spans:
                x = st_ref[h, start:start + rows, :] - maxes[h]
                pt = jnp.exp2(x.astype(BF16))
                part = _dot(vt_ref[lanes, start:start + rows], pt)
                acc = part if acc is None else acc + part
            accs.append(acc)
        pairs = []
        for p in range(MLA_HEADS // 2):
            acc_e = accs[2 * p]
            acc_o = accs[2 * p + 1]
            ot = jnp.where(low, acc_e / acc_e[HEAD64:HEAD64 + 1, :], acc_o / acc_o[0:1, :])
            sq = ot * ot
            ms_e = jnp.sum(sq[:HEAD64], axis=0, keepdims=True) * (1.0 / HEAD64)
            ms_o = jnp.sum(sq[HEAD64:], axis=0, keepdims=True) * (1.0 / HEAD64)
            ot = ot * lax.rsqrt(jnp.where(low, ms_e, ms_o) + EPS)
            pairs.append(ot.T)
        o_ref[cols, :] = (jnp.concatenate(pairs, axis=-1) * gain).astype(o_ref.dtype)


def _mla_attn(qt, k, vt, out_gain, batch, seq):
    t = k.shape[0]
    return pl.pallas_call(
        _mla_attn_kernel,
        grid=(batch,),
        in_specs=[
            pl.BlockSpec((None, MLA_PAD, seq), lambda bi: (bi, 0, 0)),
            pl.BlockSpec((seq, MLA_PAD), lambda bi: (bi, 0)),
            pl.BlockSpec((None, MLA_PAD, seq), lambda bi: (bi, 0, 0)),
            pl.BlockSpec((1, MLA_WIDTH), lambda bi: (0, 0)),
        ],
        out_specs=pl.BlockSpec((seq, MLA_WIDTH), lambda bi: (bi, 0)),
        out_shape=jax.ShapeDtypeStruct((t, MLA_WIDTH), BF16),
        scratch_shapes=[pltpu.VMEM((MLA_HEADS, seq, TQ), F32)],
        compiler_params=_params("parallel"),
        name="mla_attn",
    )(qt, k, vt, out_gain)


def _out_ffn_kernel(x_ref, ya_ref, yb_ref, yc_ref, woa_ref, wob_ref, woc_ref, g_ref,
                    w1_ref, w2_ref, o_ref, h_ref):
    j = pl.program_id(1)

    @pl.when(j == 0)
    def _():
        xn = (x_ref[...] + _dot(ya_ref[...], woa_ref[...].astype(BF16))
              + _dot(yb_ref[...], wob_ref[...].astype(BF16))
              + _dot(yc_ref[...], woc_ref[...].astype(BF16)))
        ms = jnp.mean(xn * xn, axis=-1, keepdims=True)
        h_ref[...] = (xn * lax.rsqrt(ms + EPS) * g_ref[...]).astype(h_ref.dtype)
        o_ref[...] = xn

    a = jnp.maximum(_dot(h_ref[...], w1_ref[...].astype(BF16)), 0.0)
    o_ref[...] += _dot((a * a).astype(BF16), w2_ref[...].astype(BF16))


def _out_ffn(x, ya, yb, yc, w_out_stack, gain2, w1_stack, w2_stack, layer):
    t = x.shape[0]
    wa, wb, wc = ya.shape[1], yb.shape[1], yc.shape[1]
    assert wa == wb and wc == 2 * wa
    row = lambda width: pl.BlockSpec((TM_FFN, width), lambda i, j: (i, 0))
    wo_rows = lambda width, blk: pl.BlockSpec((None, width, D_MODEL), lambda i, j: (layer, blk, 0))
    return pl.pallas_call(
        _out_ffn_kernel,
        grid=(t // TM_FFN, D_FF // TF_FFN),
        in_specs=[
            row(D_MODEL), row(wa), row(wb), row(wc),
            wo_rows(wa, 0), wo_rows(wb, 1), wo_rows(wc, 1),
            pl.BlockSpec((1, D_MODEL), lambda i, j: (0, 0)),
            pl.BlockSpec((None, D_MODEL, TF_FFN), lambda i, j: (layer, 0, j)),
            pl.BlockSpec((None, TF_FFN, D_MODEL), lambda i, j: (layer, j, 0)),
        ],
        out_specs=row(D_MODEL),
        out_shape=jax.ShapeDtypeStruct((t, D_MODEL), F32),
        scratch_shapes=[pltpu.VMEM((TM_FFN, D_MODEL), BF16)],
        compiler_params=_params("parallel", "arbitrary"),
        name="out_ffn",
    )(x, ya, yb, yc, w_out_stack, w_out_stack, w_out_stack, gain2, w1_stack, w2_stack)


def kernel(x, positions, norm1_gain, w_in, gm_v_gain, gm_w_s, gm_b_s, gm_out_gain,
           hg_lower_bound, hg_out_gain, mla_q_a_gain, mla_w_uq, mla_kv_a_gain, mla_w_ukv,
           mla_q_gain, mla_k_gain, mla_out_gain, w_out, norm2_gain, w_ff1, w_ff2):
    batch, seq, _ = x.shape
    t = batch * seq
    depth = w_in.shape[0]
    gm_w = GM_HEADS * HEAD64

    w_in_p = jnp.pad(w_in.astype(BF16), ((0, 0), (0, 0), (0, D_IN_PAD - w_in.shape[2])))

    half = QK_ROPE // 2
    wqt = jnp.swapaxes(mla_w_uq, 1, 2).astype(BF16)
    wkv = mla_w_ukv.astype(BF16).reshape(depth, KV_LORA, MLA_HEADS, QK_NOPE + HEAD64)
    wk = jnp.pad(wkv[..., :QK_NOPE], ((0, 0),) * 3 + ((0, LANES - QK_NOPE),))
    wk = wk.reshape(depth, KV_LORA, MLA_PAD)
    wvt = jnp.swapaxes(wkv[..., QK_NOPE:].reshape(depth, KV_LORA, MLA_WIDTH), 1, 2)
    q_gain_b = jnp.broadcast_to((mla_q_gain * (QK_DIM ** -0.5 * LOG2_E))[:, :, None],
                                (depth, QK_DIM, TM_MLA))
    pad_lanes = ((0, 0), (0, LANES - QK_DIM))
    k_gain_p = jnp.pad(mla_k_gain, pad_lanes)
    k_gain_sw = jnp.pad(jnp.concatenate(
        [mla_k_gain[:, :QK_NOPE], mla_k_gain[:, QK_NOPE + half:], mla_k_gain[:, QK_NOPE:QK_NOPE + half]],
        axis=-1), pad_lanes)

    bias_full = jnp.broadcast_to(jnp.swapaxes(gm_b_s, 1, 2)[..., None],
                                 (depth, CHUNK, GM_HEADS, HEAD64)).reshape(depth, CHUNK, gm_w)

    inv_freq = ROPE_THETA ** (-jnp.arange(half, dtype=F32) / half)
    cos_t, sin_t, nsin_t = _rope_tables(positions.reshape(1, t), inv_freq[:, None])
    cos_n = jnp.concatenate([jnp.ones((t, QK_NOPE), F32), cos_t.T, cos_t.T,
                             jnp.ones((t, LANES - QK_DIM), F32)], axis=1)
    sin_n = jnp.concatenate([jnp.zeros((t, QK_NOPE), F32), nsin_t.T, sin_t.T,
                             jnp.zeros((t, LANES - QK_DIM), F32)], axis=1)
    tables = (cos_n, sin_n, cos_t, sin_t)

    xs = x.reshape(t, D_MODEL)
    for l in range(depth):
        proj = _in_proj(xs, norm1_gain[l][None, :], w_in_p, l)
        y_a = _gm_mixer(proj, gm_v_gain[l][None, :], gm_w_s[l], bias_full[l],
                        gm_out_gain[l][None, :])
        y_b = _hg_mixer(proj, hg_lower_bound, hg_out_gain[l][None, :], l, batch, seq)
        qt, k, vt = _mla_prep(proj, tables, mla_q_a_gain[l][None, :], mla_kv_a_gain[l][None, :],
                              wqt[l], wk[l], wvt[l], q_gain_b[l], k_gain_p[l][None, :],
                              k_gain_sw[l][None, :], batch, seq)
        y_c = _mla_attn(qt, k, vt, mla_out_gain[l][None, :], batch, seq)
        xs = _out_ffn(xs, y_a, y_b, y_c, w_out, norm2_gain[l][None, :], w_ff1, w_ff2, l)
    return xs.reshape(batch, seq, D_MODEL)
```

```python
import functools

import jax
import jax.numpy as jnp
from jax import lax
from jax.experimental import pallas as pl
from jax.experimental.pallas import tpu as pltpu

F32 = jnp.float32
BF16 = jnp.bfloat16

D_MODEL = 1024
DEPTH = 4
CHUNK = 128
EPS = 1e-6
GM_HEADS = 4
HG_HEADS = 4
HEAD64 = 64
PAIR = 2 * HEAD64
MLA_HEADS = 8
QK_NOPE = 64
QK_ROPE = 32
QK_DIM = QK_NOPE + QK_ROPE
Q_LORA = 256
KV_LORA = 128
MLA_WIDTH = MLA_HEADS * HEAD64
ROPE_THETA = 10000.0
LOG2_E = 1.4426950408889634
D_FF = 4 * D_MODEL
LANES = 128
D_IN_PAD = 2048
GM_COLS = 2 * GM_HEADS * HEAD64
MLA_COL = 4 * HG_HEADS * HEAD64
MLA_PAD = MLA_HEADS * LANES

VMEM_LIMIT = 56 * 1024 * 1024

TM_IN = 1024
HG_CHUNKS_PER_STEP = 16
TM_MLA = 2048
TQ = 256
TM_FFN = 1024
TF_FFN = 1024


def _params(*sem):
    return pltpu.CompilerParams(dimension_semantics=sem, vmem_limit_bytes=VMEM_LIMIT)


def _dot(a, b):
    return jnp.dot(a, b, preferred_element_type=F32)


def _dot_nt(a, b):
    return lax.dot_general(a, b, (((1,), (1,)), ((), ())), preferred_element_type=F32)


def _dot_tn(a, b):
    return lax.dot_general(a, b, (((0,), (0,)), ((), ())), preferred_element_type=F32)


def _split_dot(x, w_bf16, parts, w_on_left=False):
    acc = None
    rem = x
    for _ in range(parts):
        piece = rem.astype(BF16)
        term = _dot(w_bf16, piece) if w_on_left else _dot(piece, w_bf16)
        acc = term if acc is None else acc + term
        rem = rem - piece.astype(F32)
    return acc


def _head_mean_matrix(width):
    r = lax.broadcasted_iota(jnp.int32, (width, width), 0) // HEAD64
    c = lax.broadcasted_iota(jnp.int32, (width, width), 1) // HEAD64
    return jnp.where(r == c, 1.0 / HEAD64, 0.0).astype(BF16)


def _head_rms(x, gmat):
    ms = _split_dot(x * x, gmat, 2)
    return x * lax.rsqrt(ms + EPS)


def _in_proj_kernel(x_ref, g_ref, w_ref, vg_ref, ws_ref, b_ref, og_ref, o_ref, ya_ref):
    x = x_ref[...]
    ms = jnp.mean(x * x, axis=-1, keepdims=True)
    h = x * lax.rsqrt(ms + EPS) * g_ref[...]
    p = _dot(h.astype(BF16), w_ref[...])
    o_ref[...] = p[:, GM_COLS:]
    width = GM_HEADS * HEAD64
    _gm_compute(p[:, :width], p[:, width:GM_COLS], vg_ref, ws_ref, b_ref, og_ref, ya_ref)


def _in_proj(x, gain, w_stack, layer, v_gain, w_s, bias_full, out_gain):
    t = x.shape[0]
    width = GM_HEADS * HEAD64
    rest = D_IN_PAD - GM_COLS
    return pl.pallas_call(
        _in_proj_kernel,
        grid=(t // TM_IN,),
        in_specs=[
            pl.BlockSpec((TM_IN, D_MODEL), lambda i: (i, 0)),
            pl.BlockSpec((1, D_MODEL), lambda i: (0, 0)),
            pl.BlockSpec((None, D_MODEL, D_IN_PAD), lambda i: (layer, 0, 0)),
            pl.BlockSpec((1, width), lambda i: (0, 0)),
            pl.BlockSpec((GM_HEADS, CHUNK, CHUNK), lambda i: (0, 0, 0)),
            pl.BlockSpec((CHUNK, width), lambda i: (0, 0)),
            pl.BlockSpec((1, width), lambda i: (0, 0)),
        ],
        out_specs=[pl.BlockSpec((TM_IN, rest), lambda i: (i, 0)),
                   pl.BlockSpec((TM_IN, width), lambda i: (i, 0))],
        out_shape=[jax.ShapeDtypeStruct((t, rest), F32),
                   jax.ShapeDtypeStruct((t, width), BF16)],
        compiler_params=_params("parallel"),
        name="in_proj",
    )(x, gain, w_stack, v_gain, w_s, bias_full, out_gain)


def _gm_compute(u_raw, v_raw, vg_ref, w_ref, b_ref, og_ref, o_ref):
    width = GM_HEADS * HEAD64
    gmat = _head_mean_matrix(width)
    u = jax.nn.gelu(u_raw)
    v = _head_rms(jax.nn.gelu(v_raw), gmat) * vg_ref[...]
    row = lax.broadcasted_iota(jnp.int32, (CHUNK, CHUNK), 0)
    col = lax.broadcasted_iota(jnp.int32, (CHUNK, CHUNK), 1)
    causal = row >= col
    lane_head = lax.broadcasted_iota(jnp.int32, (1, PAIR), 1) // HEAD64
    w_tril = [jnp.where(causal, w_ref[h], 0.0).astype(BF16) for h in range(GM_HEADS)]
    w_pair = [jnp.concatenate([w_tril[2 * p], w_tril[2 * p + 1]], axis=1)
              for p in range(GM_HEADS // 2)]
    bias = b_ref[...]
    for c in range(u_raw.shape[0] // CHUNK):
        rows = slice(c * CHUNK, (c + 1) * CHUNK)
        y_blocks = []
        for p in range(GM_HEADS // 2):
            vb = v[rows, p * PAIR:(p + 1) * PAIR]
            v_cat = jnp.concatenate(
                [jnp.where(lane_head == hh, vb, 0.0).astype(BF16) for hh in range(2)], axis=0)
            y_blocks.append(_dot(w_pair[p], v_cat))
        y = jnp.concatenate(y_blocks, axis=-1) + bias
        out = u[rows] * y
        out = _head_rms(out, gmat) * og_ref[...]
        o_ref[rows, :] = out.astype(o_ref.dtype)


def _anchor_rows(b, m):
    n = b.shape[0]
    if m >= 8:
        pieces = []
        for blk in range(n // (2 * m)):
            a = blk * 2 * m + m - 1
            pieces.append(jnp.broadcast_to(b[a:a + 1, :], (2 * m, b.shape[1])))
        return pieces[0] if len(pieces) == 1 else jnp.concatenate(pieces, axis=0)
    b3 = b.reshape(n // 8, 8, b.shape[1])
    sub = lax.broadcasted_iota(jnp.int32, (1, 8, 1), 1)
    r = None
    for blk in range(8 // (2 * m)):
        a = blk * 2 * m + m - 1
        cand = jnp.broadcast_to(b3[:, a:a + 1, :], b3.shape)
        r = cand if r is None else jnp.where(sub >= blk * 2 * m, cand, r)
    return r.reshape(n, b.shape[1])


def _hg_kernel(layer, q_ref, f_ref, i_ref, g_ref, lb_ref, og_ref, o_ref, st_ref):
    width = HG_HEADS * HEAD64
    n_pairs = HG_HEADS // 2

    @pl.when(pl.program_id(1) == 0)
    def _():
        st_ref[...] = jnp.zeros_like(st_ref)

    lbs = lb_ref[...]
    e = jnp.exp(lbs - jnp.max(lbs, axis=0, keepdims=True))
    soft = e / jnp.sum(e, axis=0, keepdims=True)
    lb = jnp.zeros((1, width), F32)
    for i in range(1, layer + 1):
        lb = lb + soft[i:i + 1, :]

    qr = q_ref[...]
    q_all = qr * jax.nn.sigmoid(qr)
    f_all = lb + (1.0 - lb) * jax.nn.sigmoid(f_ref[...])
    kk_all = 1.0 - f_all
    lf_all = jnp.log(f_all)
    v_all = i_ref[...]

    row = lax.broadcasted_iota(jnp.int32, (CHUNK, CHUNK), 0)
    col = lax.broadcasted_iota(jnp.int32, (CHUNK, CHUNK), 1)
    tri = jnp.where(row >= col, 1.0, 0.0).astype(BF16)
    lane_head = lax.broadcasted_iota(jnp.int32, (1, PAIR), 1) // HEAD64
    t_idx = lax.broadcasted_iota(jnp.int32, (CHUNK, 1), 0)
    rr = lax.broadcasted_iota(jnp.int32, (PAIR, PAIR), 0) // HEAD64
    cc = lax.broadcasted_iota(jnp.int32, (PAIR, PAIR), 1) // HEAD64
    same_head = rr == cc
    code = jnp.where(row >= col, row ^ col, -1)
    level_of = jnp.where(row >= col, 0, -1)
    upper_rows = []
    m = 1
    while m < CHUNK:
        level_of = level_of + jnp.where(code >= m, 1, 0)
        upper_rows.append((t_idx % (2 * m)) >= m)
        m *= 2
    n_levels = len(upper_rows)

    def pair(x, p):
        return x[:, p * PAIR:(p + 1) * PAIR]

    def head_only(xb, hh):
        return jnp.where(lane_head == hh, xb, 0.0).astype(BF16)

    def one_chunk(q_c, kk_c, f_c, lf_c, v_c):
        b_c = _split_dot(lf_c, tri, 3, w_on_left=True)
        o_blocks = []
        for p in range(n_pairs):
            q, kk, f, b, v = (pair(x, p) for x in (q_c, kk_c, f_c, b_c, v_c))
            def both_heads(lhs, rhs):
                stacked = jnp.concatenate([head_only(lhs, 0), head_only(lhs, 1)], axis=0)
                s2 = _dot_nt(stacked, rhs.astype(BF16))
                return s2[:CHUNK], s2[CHUNK:]

            scores = [jnp.where(level_of == 0, s, 0.0) for s in both_heads(q, kk)]
            for lvl in range(n_levels, 0, -1):
                m = 1 << (lvl - 1)
                upper = upper_rows[lvl - 1]
                if m == 1:
                    w = jnp.where(upper, f, 1.0)
                else:
                    d = b - _anchor_rows(b, m)
                    w = jnp.exp(jnp.minimum(d, -d))
                u = jnp.where(upper, q, kk) * w
                for hh, s in enumerate(both_heads(u, u)):
                    scores[hh] = jnp.where(level_of == lvl, s, scores[hh])

            b_last = b[CHUNK - 1:CHUNK, :]
            st = st_ref[p]
            o = _dot_nt((q * jnp.exp(b)).astype(BF16), st.astype(BF16))
            s_cat = jnp.concatenate([scores[0].astype(BF16), scores[1].astype(BF16)], axis=1)
            v_cat = jnp.concatenate([head_only(v, 0), head_only(v, 1)], axis=0)
            o_blocks.append(o + _dot(s_cat, v_cat))
            kd = kk * jnp.exp(jnp.minimum(b_last - b, 0.0))
            upd = _dot_tn(v.astype(BF16), kd.astype(BF16))
            st_ref[p] = st * jnp.exp(b_last) + jnp.where(same_head, upd, 0.0)
        return jnp.concatenate(o_blocks, axis=-1)

    outs = []
    for c in range(HG_CHUNKS_PER_STEP):
        rows = slice(c * CHUNK, (c + 1) * CHUNK)
        outs.append(one_chunk(q_all[rows], kk_all[rows], f_all[rows], lf_all[rows], v_all[rows]))
    o = jnp.concatenate(outs, axis=0)

    gmat = _head_mean_matrix(width)
    gr = g_ref[...]
    out = _head_rms(o, gmat) * og_ref[...] * (gr * jax.nn.sigmoid(gr))
    o_ref[...] = out.astype(o_ref.dtype)


def _hg_mixer(proj, lower_bound_raw, out_gain, layer, batch, seq):
    t = proj.shape[0]
    width = HG_HEADS * HEAD64
    rows = HG_CHUNKS_PER_STEP * CHUNK
    nc = seq // rows

    def col_block(j):
        return pl.BlockSpec((rows, width), lambda bi, ci: (bi * nc + ci, j))

    return pl.pallas_call(
        functools.partial(_hg_kernel, layer),
        grid=(batch, nc),
        in_specs=[
            col_block(0), col_block(1), col_block(2), col_block(3),
            pl.BlockSpec((DEPTH, width), lambda bi, ci: (0, 0)),
            pl.BlockSpec((1, width), lambda bi, ci: (0, 0)),
        ],
        out_specs=pl.BlockSpec((rows, width), lambda bi, ci: (bi * nc + ci, 0)),
        out_shape=jax.ShapeDtypeStruct((t, width), BF16),
        scratch_shapes=[pltpu.VMEM((HG_HEADS // 2, PAIR, PAIR), F32)],
        compiler_params=_params("parallel", "arbitrary"),
        name="hg_mixer",
    )(proj, proj, proj, proj, lower_bound_raw, out_gain)


def _rope_table_kernel(pos_ref, invf_ref, cos_t_ref, sin_t_ref, nsin_t_ref):
    ang_t = invf_ref[...] * pos_ref[...].astype(F32)
    sin_t = jnp.sin(ang_t)
    cos_t_ref[...] = jnp.cos(ang_t)
    sin_t_ref[...] = sin_t
    nsin_t_ref[...] = -sin_t


def _rope_tables(pos_row, invf_col):
    t = pos_row.shape[1]
    half = QK_ROPE // 2
    tm = 2048
    table = pl.BlockSpec((half, tm), lambda i: (0, i))
    return pl.pallas_call(
        _rope_table_kernel,
        grid=(t // tm,),
        in_specs=[pl.BlockSpec((1, tm), lambda i: (0, i)), pl.BlockSpec((half, 1), lambda i: (0, 0))],
        out_specs=[table] * 3,
        out_shape=[jax.ShapeDtypeStruct((half, t), F32)] * 3,
        compiler_params=_params("parallel"),
        name="rope_tables",
    )(pos_row, invf_col)


def _swap_rope_halves(x):
    lane = lax.broadcasted_iota(jnp.int32, (1, LANES), 1)
    half = QK_ROPE // 2
    from_above = pltpu.roll(x, LANES - half, axis=1)
    from_below = pltpu.roll(x, half, axis=1)
    return jnp.where(lane < QK_NOPE + half, from_above, from_below)


def _mla_prep_kernel(cq_ref, ckv_ref, kpe_ref, cos_ref, sin_ref, cos_t_ref, sin_t_ref,
                     qag_ref, kvag_ref, wqt_ref, wk_ref, wvt_ref, qg_ref, kg_ref, kgs_ref,
                     qt_out, k_out, vt_out):
    def rms(x, gain):
        ms = jnp.mean(x * x, axis=-1, keepdims=True)
        return x * lax.rsqrt(ms + EPS) * gain

    half = QK_ROPE // 2
    hq = rms(cq_ref[...], qag_ref[...]).astype(BF16)
    hkv = rms(ckv_ref[...], kvag_ref[...]).astype(BF16)

    qt = _dot_nt(wqt_ref[...], hq)
    cos_t = cos_t_ref[...]
    sin_t = sin_t_ref[...]
    for h in range(MLA_HEADS):
        x = qt[h * QK_DIM:(h + 1) * QK_DIM, :]
        ms = jnp.sum(x * x, axis=0, keepdims=True) * (1.0 / QK_DIM)
        xn = x * lax.rsqrt(ms + EPS) * qg_ref[...]
        r1 = xn[QK_NOPE:QK_NOPE + half]
        r2 = xn[QK_NOPE + half:]
        base = h * LANES
        qt_out[base:base + QK_NOPE, :] = xn[:QK_NOPE].astype(qt_out.dtype)
        qt_out[base + QK_NOPE:base + QK_NOPE + half, :] = (r1 * cos_t - r2 * sin_t).astype(qt_out.dtype)
        qt_out[base + QK_NOPE + half:base + QK_DIM, :] = (r2 * cos_t + r1 * sin_t).astype(qt_out.dtype)
        qt_out[base + QK_DIM:base + LANES, :] = jnp.zeros((LANES - QK_DIM, x.shape[1]), qt_out.dtype)

    vt = _dot_nt(wvt_ref[...], hkv).astype(vt_out.dtype)
    ones = jnp.ones((HEAD64, vt.shape[1]), vt_out.dtype)
    for h in range(MLA_HEADS):
        vals = vt[h * HEAD64:(h + 1) * HEAD64, :]
        lo, hi = (vals, ones) if h % 2 == 0 else (ones, vals)
        vt_out[h * LANES:h * LANES + HEAD64, :] = lo
        vt_out[h * LANES + HEAD64:(h + 1) * LANES, :] = hi

    kn = _dot(hkv, wk_ref[...])
    kpe = pltpu.roll(kpe_ref[...], QK_NOPE, axis=1)
    gain_cos = kg_ref[...] * cos_ref[...]
    rot = _swap_rope_halves(kpe) * kgs_ref[...] * sin_ref[...]
    for h in range(MLA_HEADS):
        lanes = slice(h * LANES, (h + 1) * LANES)
        x = kn[:, lanes] + kpe
        ms = jnp.sum(x * x, axis=-1, keepdims=True) * (1.0 / QK_DIM)
        k_out[:, lanes] = (lax.rsqrt(ms + EPS) * (x * gain_cos + rot)).astype(k_out.dtype)


def _mla_prep(proj, tables, qa_gain, kva_gain, wqt, wk, wvt, q_gain_b, k_gain, k_gain_sw,
              batch, seq):
    t = proj.shape[0]
    half = QK_ROPE // 2
    tiles_per_seq = seq // TM_MLA
    cos_n, sin_n, cos_t, sin_t = tables
    full = lambda shape: pl.BlockSpec(shape, lambda i: (0,) * len(shape))
    out = jax.ShapeDtypeStruct((t, MLA_PAD), BF16)
    out_t = jax.ShapeDtypeStruct((batch, MLA_PAD, seq), BF16)
    spec_t = pl.BlockSpec((None, MLA_PAD, TM_MLA),
                          lambda i: (i // tiles_per_seq, 0, i % tiles_per_seq))
    return pl.pallas_call(
        _mla_prep_kernel,
        grid=(t // TM_MLA,),
        in_specs=[
            pl.BlockSpec((TM_MLA, Q_LORA), lambda i: (i, MLA_COL // Q_LORA)),
            pl.BlockSpec((TM_MLA, KV_LORA), lambda i: (i, (MLA_COL + Q_LORA) // KV_LORA)),
            pl.BlockSpec((TM_MLA, LANES), lambda i: (i, (MLA_COL + Q_LORA + KV_LORA) // LANES)),
            pl.BlockSpec((TM_MLA, LANES), lambda i: (i, 0)),
            pl.BlockSpec((TM_MLA, LANES), lambda i: (i, 0)),
            pl.BlockSpec((half, TM_MLA), lambda i: (0, i)),
            pl.BlockSpec((half, TM_MLA), lambda i: (0, i)),
            full((1, Q_LORA)), full((1, KV_LORA)),
            full((MLA_HEADS * QK_DIM, Q_LORA)), full((KV_LORA, MLA_PAD)),
            full((MLA_WIDTH, KV_LORA)),
            full((QK_DIM, TM_MLA)), full((1, LANES)), full((1, LANES)),
        ],
        out_specs=[spec_t, pl.BlockSpec((TM_MLA, MLA_PAD), lambda i: (i, 0)), spec_t],
        out_shape=[out_t, out, out_t],
        compiler_params=_params("parallel"),
        name="mla_prep",
    )(proj, proj, proj, cos_n, sin_n, cos_t, sin_t, qa_gain, kva_gain, wqt, wk, wvt,
      q_gain_b, k_gain, k_gain_sw)


def _mla_attn_kernel(qt_ref, k_ref, vt_ref, og_ref, o_ref, st_ref):
    seq = k_ref.shape[0]
    row = lax.broadcasted_iota(jnp.int32, (TQ, TQ), 0)
    col = lax.broadcasted_iota(jnp.int32, (TQ, TQ), 1)
    causal_t = col >= row
    low = lax.broadcasted_iota(jnp.int32, (LANES, 1), 0) < HEAD64
    gain = og_ref[...]

    for qi in range(seq // TQ):
        cols = slice(qi * TQ, (qi + 1) * TQ)
        n_keys = (qi + 1) * TQ
        spans = [(s, min(2 * TQ, n_keys - s)) for s in range(0, n_keys, 2 * TQ)]
        maxes = [None] * MLA_HEADS
        for start, rows in spans:
            for h in range(MLA_HEADS):
                lanes = slice(h * LANES, (h + 1) * LANES)
                st = _dot(k_ref[start:start + rows, lanes], qt_ref[lanes, cols])
                if start + rows == n_keys:
                    last = jnp.where(causal_t, st[rows - TQ:], -jnp.inf)
                    st = last if rows == TQ else jnp.concatenate([st[:rows - TQ], last], axis=0)
                st_ref[h, start:start + rows, :] = st
                cm = jnp.max(st, axis=0, keepdims=True)
                maxes[h] = cm if maxes[h] is None else jnp.maximum(maxes[h], cm)
        accs = []
        for h in range(MLA_HEADS):
            lanes = slice(h * LANES, (h + 1) * LANES)
            acc = None
            for start, rows in spans:
                x = st_ref[h, start:start + rows, :] - maxes[h]
                pt = jnp.exp2(x.astype(BF16))
                part = _dot(vt_ref[lanes, start:start + rows], pt)
                acc = part if acc is None else acc + part
            accs.append(acc)
        pairs = []
        for p in range(MLA_HEADS // 2):
            acc_e = accs[2 * p]
            acc_o = accs[2 * p + 1]
            ot = jnp.where(low, acc_e / acc_e[HEAD64:HEAD64 + 1, :], acc_o / acc_o[0:1, :])
            sq = ot * ot
            ms_e = jnp.sum(sq[:HEAD64], axis=0, keepdims=True) * (1.0 / HEAD64)
            ms_o = jnp.sum(sq[HEAD64:], axis=0, keepdims=True) * (1.0 / HEAD64)
            ot = ot * lax.rsqrt(jnp.where(low, ms_e, ms_o) + EPS)
            pairs.append(ot.T)
        o_ref[cols, :] = (jnp.concatenate(pairs, axis=-1) * gain).astype(o_ref.dtype)


def _mla_attn(qt, k, vt, out_gain, batch, seq):
    t = k.shape[0]
    return pl.pallas_call(
        _mla_attn_kernel,
        grid=(batch,),
        in_specs=[
            pl.BlockSpec((None, MLA_PAD, seq), lambda bi: (bi, 0, 0)),
            pl.BlockSpec((seq, MLA_PAD), lambda bi: (bi, 0)),
            pl.BlockSpec((None, MLA_PAD, seq), lambda bi: (bi, 0, 0)),
            pl.BlockSpec((1, MLA_WIDTH), lambda bi: (0, 0)),
        ],
        out_specs=pl.BlockSpec((seq, MLA_WIDTH), lambda bi: (bi, 0)),
        out_shape=jax.ShapeDtypeStruct((t, MLA_WIDTH), BF16),
        scratch_shapes=[pltpu.VMEM((MLA_HEADS, seq, TQ), F32)],
        compiler_params=_params("parallel"),
        name="mla_attn",
    )(qt, k, vt, out_gain)


def _out_ffn_kernel(x_ref, ya_ref, yb_ref, yc_ref, woa_ref, wob_ref, woc_ref, g_ref,
                    w1_ref, w2_ref, o_ref, h_ref):
    j = pl.program_id(1)

    @pl.when(j == 0)
    def _():
        xn = (x_ref[...] + _dot(ya_ref[...], woa_ref[...].astype(BF16))
              + _dot(yb_ref[...], wob_ref[...].astype(BF16))
              + _dot(yc_ref[...], woc_ref[...].astype(BF16)))
        ms = jnp.mean(xn * xn, axis=-1, keepdims=True)
        h_ref[...] = (xn * lax.rsqrt(ms + EPS) * g_ref[...]).astype(h_ref.dtype)
        o_ref[...] = xn

    a = jnp.maximum(_dot(h_ref[...], w1_ref[...].astype(BF16)), 0.0)
    o_ref[...] += _dot((a * a).astype(BF16), w2_ref[...].astype(BF16))


def _out_ffn(x, ya, yb, yc, w_out_stack, gain2, w1_stack, w2_stack, layer):
    t = x.shape[0]
    wa, wb, wc = ya.shape[1], yb.shape[1], yc.shape[1]
    assert wa == wb and wc == 2 * wa
    row = lambda width: pl.BlockSpec((TM_FFN, width), lambda i, j: (i, 0))
    wo_rows = lambda width, blk: pl.BlockSpec((None, width, D_MODEL), lambda i, j: (layer, blk, 0))
    return pl.pallas_call(
        _out_ffn_kernel,
        grid=(t // TM_FFN, D_FF // TF_FFN),
        in_specs=[
            row(D_MODEL), row(wa), row(wb), row(wc),
            wo_rows(wa, 0), wo_rows(wb, 1), wo_rows(wc, 1),
            pl.BlockSpec((1, D_MODEL), lambda i, j: (0, 0)),
            pl.BlockSpec((None, D_MODEL, TF_FFN), lambda i, j: (layer, 0, j)),
            pl.BlockSpec((None, TF_FFN, D_MODEL), lambda i, j: (layer, j, 0)),
        ],
        out_specs=row(D_MODEL),
        out_shape=jax.ShapeDtypeStruct((t, D_MODEL), F32),
        scratch_shapes=[pltpu.VMEM((TM_FFN, D_MODEL), BF16)],
        compiler_params=_params("parallel", "arbitrary"),
        name="out_ffn",
    )(x, ya, yb, yc, w_out_stack, w_out_stack, w_out_stack, gain2, w1_stack, w2_stack)


def kernel(x, positions, norm1_gain, w_in, gm_v_gain, gm_w_s, gm_b_s, gm_out_gain,
           hg_lower_bound, hg_out_gain, mla_q_a_gain, mla_w_uq, mla_kv_a_gain, mla_w_ukv,
           mla_q_gain, mla_k_gain, mla_out_gain, w_out, norm2_gain, w_ff1, w_ff2):
    batch, seq, _ = x.shape
    t = batch * seq
    depth = w_in.shape[0]
    gm_w = GM_HEADS * HEAD64

    w_in_p = jnp.pad(w_in.astype(BF16), ((0, 0), (0, 0), (0, D_IN_PAD - w_in.shape[2])))

    half = QK_ROPE // 2
    wqt = jnp.swapaxes(mla_w_uq, 1, 2).astype(BF16)
    wkv = mla_w_ukv.astype(BF16).reshape(depth, KV_LORA, MLA_HEADS, QK_NOPE + HEAD64)
    wk = jnp.pad(wkv[..., :QK_NOPE], ((0, 0),) * 3 + ((0, LANES - QK_NOPE),))
    wk = wk.reshape(depth, KV_LORA, MLA_PAD)
    wvt = jnp.swapaxes(wkv[..., QK_NOPE:].reshape(depth, KV_LORA, MLA_WIDTH), 1, 2)
    q_gain_b = jnp.broadcast_to((mla_q_gain * (QK_DIM ** -0.5 * LOG2_E))[:, :, None],
                                (depth, QK_DIM, TM_MLA))
    pad_lanes = ((0, 0), (0, LANES - QK_DIM))
    k_gain_p = jnp.pad(mla_k_gain, pad_lanes)
    k_gain_sw = jnp.pad(jnp.concatenate(
        [mla_k_gain[:, :QK_NOPE], mla_k_gain[:, QK_NOPE + half:], mla_k_gain[:, QK_NOPE:QK_NOPE + half]],
        axis=-1), pad_lanes)

    bias_full = jnp.broadcast_to(jnp.swapaxes(gm_b_s, 1, 2)[..., None],
                                 (depth, CHUNK, GM_HEADS, HEAD64)).reshape(depth, CHUNK, gm_w)

    inv_freq = ROPE_THETA ** (-jnp.arange(half, dtype=F32) / half)
    cos_t, sin_t, nsin_t = _rope_tables(positions.reshape(1, t), inv_freq[:, None])
    cos_n = jnp.concatenate([jnp.ones((t, QK_NOPE), F32), cos_t.T, cos_t.T,
                             jnp.ones((t, LANES - QK_DIM), F32)], axis=1)
    sin_n = jnp.concatenate([jnp.zeros((t, QK_NOPE), F32), nsin_t.T, sin_t.T,
                             jnp.zeros((t, LANES - QK_DIM), F32)], axis=1)
    tables = (cos_n, sin_n, cos_t, sin_t)

    xs = x.reshape(t, D_MODEL)
    for l in range(depth):
        proj, y_a = _in_proj(xs, norm1_gain[l][None, :], w_in_p, l, gm_v_gain[l][None, :],
                             gm_w_s[l], bias_full[l], gm_out_gain[l][None, :])
        y_b = _hg_mixer(proj, hg_lower_bound, hg_out_gain[l][None, :], l, batch, seq)
        qt, k, vt = _mla_prep(proj, tables, mla_q_a_gain[l][None, :], mla_kv_a_gain[l][None, :],
                              wqt[l], wk[l], wvt[l], q_gain_b[l], k_gain_p[l][None, :],
                              k_gain_sw[l][None, :], batch, seq)
        y_c = _mla_attn(qt, k, vt, mla_out_gain[l][None, :], batch, seq)
        xs = _out_ffn(xs, y_a, y_b, y_c, w_out, norm2_gain[l][None, :], w_ff1, w_ff2, l)
    return xs.reshape(batch, seq, D_MODEL)
```

```python
import functools

import jax
import jax.numpy as jnp
from jax import lax
from jax.experimental import pallas as pl
from jax.experimental.pallas import tpu as pltpu

F32 = jnp.float32
BF16 = jnp.bfloat16

D_MODEL = 1024
DEPTH = 4
CHUNK = 128
EPS = 1e-6
GM_HEADS = 4
HG_HEADS = 4
HEAD64 = 64
PAIR = 2 * HEAD64
MLA_HEADS = 8
QK_NOPE = 64
QK_ROPE = 32
QK_DIM = QK_NOPE + QK_ROPE
Q_LORA = 256
KV_LORA = 128
MLA_WIDTH = MLA_HEADS * HEAD64
ROPE_THETA = 10000.0
LOG2_E = 1.4426950408889634
D_FF = 4 * D_MODEL
LANES = 128
D_IN_PAD = 2048
GM_COLS = 2 * GM_HEADS * HEAD64
MIX_COLS = GM_COLS + 4 * HG_HEADS * HEAD64
MLA_COL = 0
MLA_PAD = MLA_HEADS * LANES

VMEM_LIMIT = 56 * 1024 * 1024

TM_IN = 1024
TM_MLA = 2048
TQ = 256
TM_FFN = 1024
TF_FFN = 1024


def _params(*sem):
    return pltpu.CompilerParams(dimension_semantics=sem, vmem_limit_bytes=VMEM_LIMIT)


def _dot(a, b):
    return jnp.dot(a, b, preferred_element_type=F32)


def _dot_nt(a, b):
    return lax.dot_general(a, b, (((1,), (1,)), ((), ())), preferred_element_type=F32)


def _dot_tn(a, b):
    return lax.dot_general(a, b, (((0,), (0,)), ((), ())), preferred_element_type=F32)


def _split_dot(x, w_bf16, parts, w_on_left=False):
    acc = None
    rem = x
    for _ in range(parts):
        piece = rem.astype(BF16)
        term = _dot(w_bf16, piece) if w_on_left else _dot(piece, w_bf16)
        acc = term if acc is None else acc + term
        rem = rem - piece.astype(F32)
    return acc


def _head_mean_matrix(width):
    r = lax.broadcasted_iota(jnp.int32, (width, width), 0) // HEAD64
    c = lax.broadcasted_iota(jnp.int32, (width, width), 1) // HEAD64
    return jnp.where(r == c, 1.0 / HEAD64, 0.0).astype(BF16)


def _head_rms(x, gmat):
    ms = _split_dot(x * x, gmat, 2)
    return x * lax.rsqrt(ms + EPS)


def _in_proj_kernel(layer, tiles_per_seq, x_ref, g_ref, w_ref, vg_ref, ws_ref, b_ref, og_ref,
                    lb_ref, hog_ref, o_ref, ya_ref, yb_ref, st_ref):
    x = x_ref[...]
    ms = jnp.mean(x * x, axis=-1, keepdims=True)
    h = x * lax.rsqrt(ms + EPS) * g_ref[...]
    p = _dot(h.astype(BF16), w_ref[...])
    o_ref[...] = p[:, MIX_COLS:]
    width = GM_HEADS * HEAD64
    _gm_compute(p[:, :width], p[:, width:GM_COLS], vg_ref, ws_ref, b_ref, og_ref, ya_ref)
    hq, hf, hi, hg = (p[:, GM_COLS + j * width:GM_COLS + (j + 1) * width] for j in range(4))
    first = pl.program_id(0) % tiles_per_seq == 0
    _hg_compute(layer, first, hq, hf, hi, hg, lb_ref, hog_ref, yb_ref, st_ref)


def _in_proj(x, gain, w_stack, layer, seq, v_gain, w_s, bias_full, gm_out_gain,
             lower_bound_raw, hg_out_gain):
    t = x.shape[0]
    width = GM_HEADS * HEAD64
    rest = D_IN_PAD - MIX_COLS
    row_vec = pl.BlockSpec((1, width), lambda i: (0, 0))
    return pl.pallas_call(
        functools.partial(_in_proj_kernel, layer, seq // TM_IN),
        grid=(t // TM_IN,),
        in_specs=[
            pl.BlockSpec((TM_IN, D_MODEL), lambda i: (i, 0)),
            pl.BlockSpec((1, D_MODEL), lambda i: (0, 0)),
            pl.BlockSpec((None, D_MODEL, D_IN_PAD), lambda i: (layer, 0, 0)),
            row_vec,
            pl.BlockSpec((GM_HEADS, CHUNK, CHUNK), lambda i: (0, 0, 0)),
            pl.BlockSpec((CHUNK, width), lambda i: (0, 0)),
            row_vec,
            pl.BlockSpec((DEPTH, width), lambda i: (0, 0)),
            row_vec,
        ],
        out_specs=[pl.BlockSpec((TM_IN, rest), lambda i: (i, 0)),
                   pl.BlockSpec((TM_IN, width), lambda i: (i, 0)),
                   pl.BlockSpec((TM_IN, width), lambda i: (i, 0))],
        out_shape=[jax.ShapeDtypeStruct((t, rest), F32),
                   jax.ShapeDtypeStruct((t, width), BF16),
                   jax.ShapeDtypeStruct((t, width), BF16)],
        scratch_shapes=[pltpu.VMEM((HG_HEADS // 2, PAIR, PAIR), F32)],
        compiler_params=_params("arbitrary"),
        name="in_proj",
    )(x, gain, w_stack, v_gain, w_s, bias_full, gm_out_gain, lower_bound_raw, hg_out_gain)


def _gm_compute(u_raw, v_raw, vg_ref, w_ref, b_ref, og_ref, o_ref):
    width = GM_HEADS * HEAD64
    gmat = _head_mean_matrix(width)
    u = jax.nn.gelu(u_raw)
    v = _head_rms(jax.nn.gelu(v_raw), gmat) * vg_ref[...]
    row = lax.broadcasted_iota(jnp.int32, (CHUNK, CHUNK), 0)
    col = lax.broadcasted_iota(jnp.int32, (CHUNK, CHUNK), 1)
    causal = row >= col
    lane_head = lax.broadcasted_iota(jnp.int32, (1, PAIR), 1) // HEAD64
    w_tril = [jnp.where(causal, w_ref[h], 0.0).astype(BF16) for h in range(GM_HEADS)]
    w_pair = [jnp.concatenate([w_tril[2 * p], w_tril[2 * p + 1]], axis=1)
              for p in range(GM_HEADS // 2)]
    bias = b_ref[...]
    for c in range(u_raw.shape[0] // CHUNK):
        rows = slice(c * CHUNK, (c + 1) * CHUNK)
        y_blocks = []
        for p in range(GM_HEADS // 2):
            vb = v[rows, p * PAIR:(p + 1) * PAIR]
            v_cat = jnp.concatenate(
                [jnp.where(lane_head == hh, vb, 0.0).astype(BF16) for hh in range(2)], axis=0)
            y_blocks.append(_dot(w_pair[p], v_cat))
        y = jnp.concatenate(y_blocks, axis=-1) + bias
        out = u[rows] * y
        out = _head_rms(out, gmat) * og_ref[...]
        o_ref[rows, :] = out.astype(o_ref.dtype)


def _anchor_rows(b, m):
    n = b.shape[0]
    if m >= 8:
        pieces = []
        for blk in range(n // (2 * m)):
            a = blk * 2 * m + m - 1
            pieces.append(jnp.broadcast_to(b[a:a + 1, :], (2 * m, b.shape[1])))
        return pieces[0] if len(pieces) == 1 else jnp.concatenate(pieces, axis=0)
    b3 = b.reshape(n // 8, 8, b.shape[1])
    sub = lax.broadcasted_iota(jnp.int32, (1, 8, 1), 1)
    r = None
    for blk in range(8 // (2 * m)):
        a = blk * 2 * m + m - 1
        cand = jnp.broadcast_to(b3[:, a:a + 1, :], b3.shape)
        r = cand if r is None else jnp.where(sub >= blk * 2 * m, cand, r)
    return r.reshape(n, b.shape[1])


def _hg_compute(layer, first, qr, f_raw, v_all, gr, lb_ref, og_ref, o_ref, st_ref):
    width = HG_HEADS * HEAD64
    n_pairs = HG_HEADS // 2

    @pl.when(first)
    def _():
        st_ref[...] = jnp.zeros_like(st_ref)

    lbs = lb_ref[...]
    e = jnp.exp(lbs - jnp.max(lbs, axis=0, keepdims=True))
    soft = e / jnp.sum(e, axis=0, keepdims=True)
    lb = jnp.zeros((1, width), F32)
    for i in range(1, layer + 1):
        lb = lb + soft[i:i + 1, :]

    q_all = qr * jax.nn.sigmoid(qr)
    f_all = lb + (1.0 - lb) * jax.nn.sigmoid(f_raw)
    kk_all = 1.0 - f_all
    lf_all = jnp.log(f_all)

    row = lax.broadcasted_iota(jnp.int32, (CHUNK, CHUNK), 0)
    col = lax.broadcasted_iota(jnp.int32, (CHUNK, CHUNK), 1)
    tri = jnp.where(row >= col, 1.0, 0.0).astype(BF16)
    lane_head = lax.broadcasted_iota(jnp.int32, (1, PAIR), 1) // HEAD64
    t_idx = lax.broadcasted_iota(jnp.int32, (CHUNK, 1), 0)
    rr = lax.broadcasted_iota(jnp.int32, (PAIR, PAIR), 0) // HEAD64
    cc = lax.broadcasted_iota(jnp.int32, (PAIR, PAIR), 1) // HEAD64
    same_head = rr == cc
    code = jnp.where(row >= col, row ^ col, -1)
    level_of = jnp.where(row >= col, 0, -1)
    upper_rows = []
    m = 1
    while m < CHUNK:
        level_of = level_of + jnp.where(code >= m, 1, 0)
        upper_rows.append((t_idx % (2 * m)) >= m)
        m *= 2
    n_levels = len(upper_rows)

    def pair(x, p):
        return x[:, p * PAIR:(p + 1) * PAIR]

    def head_only(xb, hh):
        return jnp.where(lane_head == hh, xb, 0.0).astype(BF16)

    def one_chunk(q_c, kk_c, f_c, lf_c, v_c):
        b_c = _split_dot(lf_c, tri, 3, w_on_left=True)
        o_blocks = []
        for p in range(n_pairs):
            q, kk, f, b, v = (pair(x, p) for x in (q_c, kk_c, f_c, b_c, v_c))
            def both_heads(lhs, rhs):
                stacked = jnp.concatenate([head_only(lhs, 0), head_only(lhs, 1)], axis=0)
                s2 = _dot_nt(stacked, rhs.astype(BF16))
                return s2[:CHUNK], s2[CHUNK:]

            scores = [jnp.where(level_of == 0, s, 0.0) for s in both_heads(q, kk)]
            for lvl in range(n_levels, 0, -1):
                m = 1 << (lvl - 1)
                upper = upper_rows[lvl - 1]
                if m == 1:
                    w = jnp.where(upper, f, 1.0)
                else:
                    d = b - _anchor_rows(b, m)
                    w = jnp.exp(jnp.minimum(d, -d))
                u = jnp.where(upper, q, kk) * w
                for hh, s in enumerate(both_heads(u, u)):
                    scores[hh] = jnp.where(level_of == lvl, s, scores[hh])

            b_last = b[CHUNK - 1:CHUNK, :]
            st = st_ref[p]
            o = _dot_nt((q * jnp.exp(b)).astype(BF16), st.astype(BF16))
            s_cat = jnp.concatenate([scores[0].astype(BF16), scores[1].astype(BF16)], axis=1)
            v_cat = jnp.concatenate([head_only(v, 0), head_only(v, 1)], axis=0)
            o_blocks.append(o + _dot(s_cat, v_cat))
            kd = kk * jnp.exp(jnp.minimum(b_last - b, 0.0))
            upd = _dot_tn(v.astype(BF16), kd.astype(BF16))
            st_ref[p] = st * jnp.exp(b_last) + jnp.where(same_head, upd, 0.0)
        return jnp.concatenate(o_blocks, axis=-1)

    outs = []
    for c in range(qr.shape[0] // CHUNK):
        rows = slice(c * CHUNK, (c + 1) * CHUNK)
        outs.append(one_chunk(q_all[rows], kk_all[rows], f_all[rows], lf_all[rows], v_all[rows]))
    o = jnp.concatenate(outs, axis=0)

    gmat = _head_mean_matrix(width)
    out = _head_rms(o, gmat) * og_ref[...] * (gr * jax.nn.sigmoid(gr))
    o_ref[...] = out.astype(o_ref.dtype)


def _rope_table_kernel(pos_ref, invf_ref, cos_t_ref, sin_t_ref, nsin_t_ref):
    ang_t = invf_ref[...] * pos_ref[...].astype(F32)
    sin_t = jnp.sin(ang_t)
    cos_t_ref[...] = jnp.cos(ang_t)
    sin_t_ref[...] = sin_t
    nsin_t_ref[...] = -sin_t


def _rope_tables(pos_row, invf_col):
    t = pos_row.shape[1]
    half = QK_ROPE // 2
    tm = 2048
    table = pl.BlockSpec((half, tm), lambda i: (0, i))
    return pl.pallas_call(
        _rope_table_kernel,
        grid=(t // tm,),
        in_specs=[pl.BlockSpec((1, tm), lambda i: (0, i)), pl.BlockSpec((half, 1), lambda i: (0, 0))],
        out_specs=[table] * 3,
        out_shape=[jax.ShapeDtypeStruct((half, t), F32)] * 3,
        compiler_params=_params("parallel"),
        name="rope_tables",
    )(pos_row, invf_col)


def _swap_rope_halves(x):
    lane = lax.broadcasted_iota(jnp.int32, (1, LANES), 1)
    half = QK_ROPE // 2
    from_above = pltpu.roll(x, LANES - half, axis=1)
    from_below = pltpu.roll(x, half, axis=1)
    return jnp.where(lane < QK_NOPE + half, from_above, from_below)


def _mla_prep_kernel(cq_ref, ckv_ref, kpe_ref, cos_ref, sin_ref, cos_t_ref, sin_t_ref,
                     qag_ref, kvag_ref, wqt_ref, wk_ref, wvt_ref, qg_ref, kg_ref, kgs_ref,
                     qt_out, k_out, vt_out):
    def rms(x, gain):
        ms = jnp.mean(x * x, axis=-1, keepdims=True)
        return x * lax.rsqrt(ms + EPS) * gain

    half = QK_ROPE // 2
    hq = rms(cq_ref[...], qag_ref[...]).astype(BF16)
    hkv = rms(ckv_ref[...], kvag_ref[...]).astype(BF16)

    qt = _dot_nt(wqt_ref[...], hq)
    cos_t = cos_t_ref[...]
    sin_t = sin_t_ref[...]
    for h in range(MLA_HEADS):
        x = qt[h * QK_DIM:(h + 1) * QK_DIM, :]
        ms = jnp.sum(x * x, axis=0, keepdims=True) * (1.0 / QK_DIM)
        xn = x * lax.rsqrt(ms + EPS) * qg_ref[...]
        r1 = xn[QK_NOPE:QK_NOPE + half]
        r2 = xn[QK_NOPE + half:]
        base = h * LANES
        qt_out[base:base + QK_NOPE, :] = xn[:QK_NOPE].astype(qt_out.dtype)
        qt_out[base + QK_NOPE:base + QK_NOPE + half, :] = (r1 * cos_t - r2 * sin_t).astype(qt_out.dtype)
        qt_out[base + QK_NOPE + half:base + QK_DIM, :] = (r2 * cos_t + r1 * sin_t).astype(qt_out.dtype)
        qt_out[base + QK_DIM:base + LANES, :] = jnp.zeros((LANES - QK_DIM, x.shape[1]), qt_out.dtype)

    vt = _dot_nt(wvt_ref[...], hkv).astype(vt_out.dtype)
    ones = jnp.ones((HEAD64, vt.shape[1]), vt_out.dtype)
    for h in range(MLA_HEADS):
        vals = vt[h * HEAD64:(h + 1) * HEAD64, :]
        lo, hi = (vals, ones) if h % 2 == 0 else (ones, vals)
        vt_out[h * LANES:h * LANES + HEAD64, :] = lo
        vt_out[h * LANES + HEAD64:(h + 1) * LANES, :] = hi

    kn = _dot(hkv, wk_ref[...])
    kpe = pltpu.roll(kpe_ref[...], QK_NOPE, axis=1)
    gain_cos = kg_ref[...] * cos_ref[...]
    rot = _swap_rope_halves(kpe) * kgs_ref[...] * sin_ref[...]
    for h in range(MLA_HEADS):
        lanes = slice(h * LANES, (h + 1) * LANES)
        x = kn[:, lanes] + kpe
        ms = jnp.sum(x * x, axis=-1, keepdims=True) * (1.0 / QK_DIM)
        k_out[:, lanes] = (lax.rsqrt(ms + EPS) * (x * gain_cos + rot)).astype(k_out.dtype)


def _mla_prep(proj, tables, qa_gain, kva_gain, wqt, wk, wvt, q_gain_b, k_gain, k_gain_sw,
              batch, seq):
    t = proj.shape[0]
    half = QK_ROPE // 2
    tiles_per_seq = seq // TM_MLA
    cos_n, sin_n, cos_t, sin_t = tables
    full = lambda shape: pl.BlockSpec(shape, lambda i: (0,) * len(shape))
    out = jax.ShapeDtypeStruct((t, MLA_PAD), BF16)
    out_t = jax.ShapeDtypeStruct((batch, MLA_PAD, seq), BF16)
    spec_t = pl.BlockSpec((None, MLA_PAD, TM_MLA),
                          lambda i: (i // tiles_per_seq, 0, i % tiles_per_seq))
    return pl.pallas_call(
        _mla_prep_kernel,
        grid=(t // TM_MLA,),
        in_specs=[
            pl.BlockSpec((TM_MLA, Q_LORA), lambda i: (i, MLA_COL // Q_LORA)),
            pl.BlockSpec((TM_MLA, KV_LORA), lambda i: (i, (MLA_COL + Q_LORA) // KV_LORA)),
            pl.BlockSpec((TM_MLA, LANES), lambda i: (i, (MLA_COL + Q_LORA + KV_LORA) // LANES)),
            pl.BlockSpec((TM_MLA, LANES), lambda i: (i, 0)),
            pl.BlockSpec((TM_MLA, LANES), lambda i: (i, 0)),
            pl.BlockSpec((half, TM_MLA), lambda i: (0, i)),
            pl.BlockSpec((half, TM_MLA), lambda i: (0, i)),
            full((1, Q_LORA)), full((1, KV_LORA)),
            full((MLA_HEADS * QK_DIM, Q_LORA)), full((KV_LORA, MLA_PAD)),
            full((MLA_WIDTH, KV_LORA)),
            full((QK_DIM, TM_MLA)), full((1, LANES)), full((1, LANES)),
        ],
        out_specs=[spec_t, pl.BlockSpec((TM_MLA, MLA_PAD), lambda i: (i, 0)), spec_t],
        out_shape=[out_t, out, out_t],
        compiler_params=_params("parallel"),
        name="mla_prep",
    )(proj, proj, proj, cos_n, sin_n, cos_t, sin_t, qa_gain, kva_gain, wqt, wk, wvt,
      q_gain_b, k_gain, k_gain_sw)


def _mla_attn_kernel(qt_ref, k_ref, vt_ref, og_ref, o_ref, st_ref):
    seq = k_ref.shape[0]
    row = lax.broadcasted_iota(jnp.int32, (TQ, TQ), 0)
    col = lax.broadcasted_iota(jnp.int32, (TQ, TQ), 1)
    causal_t = col >= row
    low = lax.broadcasted_iota(jnp.int32, (LANES, 1), 0) < HEAD64
    gain = og_ref[...]

    for qi in range(seq // TQ):
        cols = slice(qi * TQ, (qi + 1) * TQ)
        n_keys = (qi + 1) * TQ
        spans = [(s, min(2 * TQ, n_keys - s)) for s in range(0, n_keys, 2 * TQ)]
        maxes = [None] * MLA_HEADS
        for start, rows in spans:
            for h in range(MLA_HEADS):
                lanes = slice(h * LANES, (h + 1) * LANES)
                st = _dot(k_ref[start:start + rows, lanes], qt_ref[lanes, cols])
                if start + rows == n_keys:
                    last = jnp.where(causal_t, st[rows - TQ:], -jnp.inf)
                    st = last if rows == TQ else jnp.concatenate([st[:rows - TQ], last], axis=0)
                st_ref[h, start:start + rows, :] = st
                cm = jnp.max(st, axis=0, keepdims=True)
                maxes[h] = cm if maxes[h] is None else jnp.maximum(maxes[h], cm)
        accs = []
        for h in range(MLA_HEADS):
            lanes = slice(h * LANES, (h + 1) * LANES)
            acc = None
            for start, rows in spans:
                x = st_ref[h, start:start + rows, :] - maxes[h]
                pt = jnp.exp2(x.astype(BF16))
                part = _dot(vt_ref[lanes, start:start + rows], pt)
                acc = part if acc is None else acc + part
            accs.append(acc)
        pairs = []
        for p in range(MLA_HEADS // 2):
            acc_e = accs[2 * p]
            acc_o = accs[2 * p + 1]
            ot = jnp.where(low, acc_e / acc_e[HEAD64:HEAD64 + 1, :], acc_o / acc_o[0:1, :])
            sq = ot * ot
            ms_e = jnp.sum(sq[:HEAD64], axis=0, keepdims=True) * (1.0 / HEAD64)
            ms_o = jnp.sum(sq[HEAD64:], axis=0, keepdims=True) * (1.0 / HEAD64)
            ot = ot * lax.rsqrt(jnp.where(low, ms_e, ms_o) + EPS)
            pairs.append(ot.T)
        o_ref[cols, :] = (jnp.concatenate(pairs, axis=-1) * gain).astype(o_ref.dtype)


def _mla_attn(qt, k, vt, out_gain, batch, seq):
    t = k.shape[0]
    return pl.pallas_call(
        _mla_attn_kernel,
        grid=(batch,),
        in_specs=[
            pl.BlockSpec((None, MLA_PAD, seq), lambda bi: (bi, 0, 0)),
            pl.BlockSpec((seq, MLA_PAD), lambda bi: (bi, 0)),
            pl.BlockSpec((None, MLA_PAD, seq), lambda bi: (bi, 0, 0)),
            pl.BlockSpec((1, MLA_WIDTH), lambda bi: (0, 0)),
        ],
        out_specs=pl.BlockSpec((seq, MLA_WIDTH), lambda bi: (bi, 0)),
        out_shape=jax.ShapeDtypeStruct((t, MLA_WIDTH), BF16),
        scratch_shapes=[pltpu.VMEM((MLA_HEADS, seq, TQ), F32)],
        compiler_params=_params("parallel"),
        name="mla_attn",
    )(qt, k, vt, out_gain)


def _out_ffn_kernel(x_ref, ya_ref, yb_ref, yc_ref, woa_ref, wob_ref, woc_ref, g_ref,
                    w1_ref, w2_ref, o_ref, h_ref):
    j = pl.program_id(1)

    @pl.when(j == 0)
    def _():
        xn = (x_ref[...] + _dot(ya_ref[...], woa_ref[...].astype(BF16))
              + _dot(yb_ref[...], wob_ref[...].astype(BF16))
              + _dot(yc_ref[...], woc_ref[...].astype(BF16)))
        ms = jnp.mean(xn * xn, axis=-1, keepdims=True)
        h_ref[...] = (xn * lax.rsqrt(ms + EPS) * g_ref[...]).astype(h_ref.dtype)
        o_ref[...] = xn

    a = jnp.maximum(_dot(h_ref[...], w1_ref[...].astype(BF16)), 0.0)
    o_ref[...] += _dot((a * a).astype(BF16), w2_ref[...].astype(BF16))


def _out_ffn(x, ya, yb, yc, w_out_stack, gain2, w1_stack, w2_stack, layer):
    t = x.shape[0]
    wa, wb, wc = ya.shape[1], yb.shape[1], yc.shape[1]
    assert wa == wb and wc == 2 * wa
    row = lambda width: pl.BlockSpec((TM_FFN, width), lambda i, j: (i, 0))
    wo_rows = lambda width, blk: pl.BlockSpec((None, width, D_MODEL), lambda i, j: (layer, blk, 0))
    return pl.pallas_call(
        _out_ffn_kernel,
        grid=(t // TM_FFN, D_FF // TF_FFN),
        in_specs=[
            row(D_MODEL), row(wa), row(wb), row(wc),
            wo_rows(wa, 0), wo_rows(wb, 1), wo_rows(wc, 1),
            pl.BlockSpec((1, D_MODEL), lambda i, j: (0, 0)),
            pl.BlockSpec((None, D_MODEL, TF_FFN), lambda i, j: (layer, 0, j)),
            pl.BlockSpec((None, TF_FFN, D_MODEL), lambda i, j: (layer, j, 0)),
        ],
        out_specs=row(D_MODEL),
        out_shape=jax.ShapeDtypeStruct((t, D_MODEL), F32),
        scratch_shapes=[pltpu.VMEM((TM_FFN, D_MODEL), BF16)],
        compiler_params=_params("parallel", "arbitrary"),
        name="out_ffn",
    )(x, ya, yb, yc, w_out_stack, w_out_stack, w_out_stack, gain2, w1_stack, w2_stack)


def kernel(x, positions, norm1_gain, w_in, gm_v_gain, gm_w_s, gm_b_s, gm_out_gain,
           hg_lower_bound, hg_out_gain, mla_q_a_gain, mla_w_uq, mla_kv_a_gain, mla_w_ukv,
           mla_q_gain, mla_k_gain, mla_out_gain, w_out, norm2_gain, w_ff1, w_ff2):
    batch, seq, _ = x.shape
    t = batch * seq
    depth = w_in.shape[0]
    gm_w = GM_HEADS * HEAD64

    w_in_p = jnp.pad(w_in.astype(BF16), ((0, 0), (0, 0), (0, D_IN_PAD - w_in.shape[2])))

    half = QK_ROPE // 2
    wqt = jnp.swapaxes(mla_w_uq, 1, 2).astype(BF16)
    wkv = mla_w_ukv.astype(BF16).reshape(depth, KV_LORA, MLA_HEADS, QK_NOPE + HEAD64)
    wk = jnp.pad(wkv[..., :QK_NOPE], ((0, 0),) * 3 + ((0, LANES - QK_NOPE),))
    wk = wk.reshape(depth, KV_LORA, MLA_PAD)
    wvt = jnp.swapaxes(wkv[..., QK_NOPE:].reshape(depth, KV_LORA, MLA_WIDTH), 1, 2)
    q_gain_b = jnp.broadcast_to((mla_q_gain * (QK_DIM ** -0.5 * LOG2_E))[:, :, None],
                                (depth, QK_DIM, TM_MLA))
    pad_lanes = ((0, 0), (0, LANES - QK_DIM))
    k_gain_p = jnp.pad(mla_k_gain, pad_lanes)
    k_gain_sw = jnp.pad(jnp.concatenate(
        [mla_k_gain[:, :QK_NOPE], mla_k_gain[:, QK_NOPE + half:], mla_k_gain[:, QK_NOPE:QK_NOPE + half]],
        axis=-1), pad_lanes)

    bias_full = jnp.broadcast_to(jnp.swapaxes(gm_b_s, 1, 2)[..., None],
                                 (depth, CHUNK, GM_HEADS, HEAD64)).reshape(depth, CHUNK, gm_w)

    inv_freq = ROPE_THETA ** (-jnp.arange(half, dtype=F32) / half)
    cos_t, sin_t, nsin_t = _rope_tables(positions.reshape(1, t), inv_freq[:, None])
    cos_n = jnp.concatenate([jnp.ones((t, QK_NOPE), F32), cos_t.T, cos_t.T,
                             jnp.ones((t, LANES - QK_DIM), F32)], axis=1)
    sin_n = jnp.concatenate([jnp.zeros((t, QK_NOPE), F32), nsin_t.T, sin_t.T,
                             jnp.zeros((t, LANES - QK_DIM), F32)], axis=1)
    tables = (cos_n, sin_n, cos_t, sin_t)

    xs = x.reshape(t, D_MODEL)
    for l in range(depth):
        proj, y_a, y_b = _in_proj(xs, norm1_gain[l][None, :], w_in_p, l, seq,
                                  gm_v_gain[l][None, :], gm_w_s[l], bias_full[l],
                                  gm_out_gain[l][None, :], hg_lower_bound,
                                  hg_out_gain[l][None, :])
        qt, k, vt = _mla_prep(proj, tables, mla_q_a_gain[l][None, :], mla_kv_a_gain[l][None, :],
                              wqt[l], wk[l], wvt[l], q_gain_b[l], k_gain_p[l][None, :],
                              k_gain_sw[l][None, :], batch, seq)
        y_c = _mla_attn(qt, k, vt, mla_out_gain[l][None, :], batch, seq)
        xs = _out_ffn(xs, y_a, y_b, y_c, w_out, norm2_gain[l][None, :], w_ff1, w_ff2, l)
    return xs.reshape(batch, seq, D_MODEL)
```
